```python
import jax, jax.numpy as jnp
from jax import lax
import numpy as np

D_MODEL = 2048
BATCH = 4
SEQ = 4096
DEPTH = 2

CHUNK = 64
D_MIX = D_MODEL
GROUP = D_MIX // 4
LN_EPS = 1e-5
DN_ALPHA = (2 * DEPTH) ** 0.25
DN_BETA = (8 * DEPTH) ** -0.25
N_MOD = 6

CONV_WIDTH = 31

MLA_HEADS = 4
MLA_NOPE = 128
MLA_ROPE = 64
MLA_VDIM = 128
MLA_Q_RANK = 384
MLA_KV_RANK = 256
ROPE_THETA = 10000.0
Q_BLOCK = 128
MAX_STREAM_OFFSET = 65536

RW_HEAD = 64
RW_HEADS = GROUP // RW_HEAD
RW_DECAY_LORA = 96
RW_AAA_LORA = 96
RW_MV_LORA = 64
RW_GATE_LORA = 256
RW_GN_EPS = 64e-5

CA_HEADS = 4
CA_HEAD_DIM = GROUP // CA_HEADS
CA_LEFT = 8
REL_CLIP = 256

N_EXPERTS = 32
TOP_K = 4
D_FF = D_MODEL
SWIGLU_LIMIT = 7.0
SWIGLU_ALPHA = 1.702
MOE_BLOCK = 256

IN_SPLITS = (2 * GROUP, MLA_Q_RANK, MLA_KV_RANK, MLA_ROPE, 3 * GROUP, 3 * GROUP)
D_IN = sum(IN_SPLITS)
IN_OFFSETS = tuple(int(o) for o in np.cumsum(IN_SPLITS)[:-1])

kernel_name = 'hybrid_streaming_encoder_block'


def layer_norm(x, g=None, b=None, eps=LN_EPS):
    xf = x.astype(jnp.float32)
    mu = xf.mean(-1, keepdims=True)
    var = jnp.square(xf - mu).mean(-1, keepdims=True)
    y = (xf - mu) * lax.rsqrt(var + eps)
    if g is not None:
        y = y * g + b
    return y.astype(x.dtype)


def rms_norm(x, g, eps=1e-6):
    xf = x.astype(jnp.float32)
    y = xf * lax.rsqrt(jnp.mean(xf * xf, -1, keepdims=True) + eps)
    return (y * g).astype(x.dtype)


def rope_tables(positions):
    inv = ROPE_THETA ** (-jnp.arange(0, MLA_ROPE, 2, dtype=jnp.float32) / MLA_ROPE)
    ang = positions.astype(jnp.float32)[..., None] * inv
    return jnp.cos(ang), jnp.sin(ang)


def apply_rope(x, cos, sin):
    x1, x2 = jnp.split(x.astype(jnp.float32), 2, axis=-1)
    return jnp.concatenate([x1 * cos - x2 * sin, x1 * sin + x2 * cos], -1).astype(x.dtype)


def token_shift(t):
    return jnp.pad(t, ((0, 0), (1, 0), (0, 0)))[:, :-1]


def conv_module(u2, dw, db, ln_g, ln_b):
    val, gate = jnp.split(u2, 2, axis=-1)
    u = val * jax.nn.sigmoid(gate)
    u = jnp.pad(u, ((0, 0), (CONV_WIDTH - 1, 0), (0, 0)))
    u = lax.conv_general_dilated(u, dw[:, None, :].astype(u.dtype), window_strides=(1,),
                                 padding='VALID', dimension_numbers=('NWC', 'WIO', 'NWC'),
                                 feature_group_count=GROUP) + db
    return jax.nn.silu(layer_norm(u, ln_g, ln_b))


def mla(cq, ckv, kpe, q_norm, kv_norm, w_uq, w_ukv, cos, sin):
    B, S, _ = cq.shape
    q = (rms_norm(cq, q_norm) @ w_uq).reshape(B, S, MLA_HEADS, MLA_NOPE + MLA_ROPE)
    q_nope, q_pe = jnp.split(q, [MLA_NOPE], axis=-1)
    q = jnp.concatenate([q_nope, apply_rope(q_pe, cos[:, :, None], sin[:, :, None])], -1)
    kv = (rms_norm(ckv, kv_norm) @ w_ukv).reshape(B, S, MLA_HEADS, MLA_NOPE + MLA_VDIM)
    k_nope, v = jnp.split(kv, [MLA_NOPE], axis=-1)
    k_pe = apply_rope(kpe, cos, sin)[:, :, None, :]
    k = jnp.concatenate([k_nope, jnp.broadcast_to(k_pe, (B, S, MLA_HEADS, MLA_ROPE))], -1)
    scale = (MLA_NOPE + MLA_ROPE) ** -0.5
    chunk_id = jnp.arange(S) // CHUNK
    nqb = S // Q_BLOCK
    qb = jnp.moveaxis(q.reshape(B, nqb, Q_BLOCK, MLA_HEADS, MLA_NOPE + MLA_ROPE), 1, 0)
    qchunk = chunk_id.reshape(nqb, Q_BLOCK)

    def query_block(args):
        qi, qc = args
        s = jnp.einsum('bqhd,bkhd->bhqk', qi, k).astype(jnp.float32) * scale
        s = jnp.where(chunk_id[None, :] <= qc[:, None], s, -jnp.inf)
        p = jax.nn.softmax(s, axis=-1).astype(v.dtype)
        return jnp.einsum('bhqk,bkhd->bqhd', p, v)

    o = lax.map(query_block, (qb, qchunk))
    return jnp.moveaxis(o, 0, 1).reshape(B, S, MLA_HEADS * MLA_VDIM)


def rwkv7_time_mix(h, r, k, v, v_first, mu_rkv, mu_x, w0, w1, w2, a0, a1, a2,
                   g1, g2, k_k, k_a, r_k, gn_g, gn_b, vres):
    B, S, _ = h.shape
    f32 = jnp.float32
    dh = token_shift(h) - h
    xw, xa, xg = h + dh * mu_x[0], h + dh * mu_x[1], h + dh * mu_x[2]
    r = r + (token_shift(r) - r) * mu_rkv[0]
    k = k + (token_shift(k) - k) * mu_rkv[1]
    v = v + (token_shift(v) - v) * mu_rkv[2]
    w = -jax.nn.softplus(-(w0 + jnp.tanh(xw @ w1) @ w2)) - 0.5
    decay = jnp.exp(-jnp.exp(w.astype(f32)))
    a = jax.nn.sigmoid(a0 + (xa @ a1) @ a2)
    g = jax.nn.sigmoid(xg @ g1) @ g2
    heads = lambda t: t.reshape(B, S, RW_HEADS, RW_HEAD)
    kk = heads(k * k_k).astype(f32)
    kk = kk / jnp.maximum(jnp.sqrt(jnp.sum(kk * kk, -1, keepdims=True)), 1e-12)
    k = k * (1 + (a - 1) * k_a)
    if vres is None:
        v_first = v
    else:
        mu_v, v0, v1, v2 = vres
        xv = h + dh * mu_v
        v = v + (v_first - v) * jax.nn.sigmoid(v0 + (xv @ v1) @ v2)
    seq = lambda t: jnp.moveaxis(heads(t).astype(f32), 1, 0)
    xs = (seq(r), seq(decay), seq(k), seq(v), jnp.moveaxis(kk, 1, 0), seq(a))

    def step(state, inp):
        rt, wt, kt, vt, kkt, at = inp
        s_kk = jnp.einsum('bhvk,bhk->bhv', state, kkt)
        state = (state * wt[:, :, None, :]
                 - s_kk[..., None] * (kkt * at)[:, :, None, :]
                 + vt[..., None] * kt[:, :, None, :])
        return state, jnp.einsum('bhvk,bhk->bhv', state, rt)

    state0 = jnp.zeros((B, RW_HEADS, RW_HEAD, RW_HEAD), f32)
    _, y = lax.scan(step, state0, xs)
    y = jnp.moveaxis(y, 0, 1)
    mu = y.mean(-1, keepdims=True)
    var = jnp.square(y - mu).mean(-1, keepdims=True)
    y = ((y - mu) * lax.rsqrt(var + RW_GN_EPS)).reshape(B, S, GROUP) * gn_g + gn_b
    bonus = jnp.sum(heads(r * k).astype(f32) * r_k, -1, keepdims=True) * heads(v).astype(f32)
    y = (y + bonus.reshape(B, S, GROUP)) * g
    return y.astype(h.dtype), v_first


def chunk_band_attention(q, k, v, rel_table):
    B, S, _ = q.shape
    NC = S // CHUNK
    W = (CA_LEFT + 1) * CHUNK
    qc = q.reshape(B, NC, CHUNK, CA_HEADS, CA_HEAD_DIM)

    def band(t):
        tc = t.reshape(B, NC, CHUNK, CA_HEADS, CA_HEAD_DIM)
        tp = jnp.pad(tc, ((0, 0), (CA_LEFT, 0), (0, 0), (0, 0), (0, 0)))
        return jnp.concatenate([tp[:, j:j + NC] for j in range(CA_LEFT + 1)], axis=2)

    kb, vb = band(k), band(v)
    q_pos = CA_LEFT * CHUNK + jnp.arange(CHUNK)[:, None]
    k_pos = jnp.arange(W)[None, :]
    rel = jnp.clip(q_pos - k_pos, -REL_CLIP, REL_CLIP) + REL_CLIP
    bias = rel_table[:, rel].astype(jnp.float32)
    key_chunk = jnp.arange(NC)[:, None] - CA_LEFT + jnp.arange(W)[None, :] // CHUNK
    valid = key_chunk >= 0
    s = jnp.einsum('bnqhd,bnkhd->bhnqk', qc, kb).astype(jnp.float32) * CA_HEAD_DIM ** -0.5
    s = s + bias[:, None]
    s = jnp.where(valid[None, None, :, None, :], s, -jnp.inf)
    p = jax.nn.softmax(s, axis=-1).astype(v.dtype)
    o = jnp.einsum('bhnqk,bnkhd->bnqhd', p, vb)
    return o.reshape(B, S, GROUP)


def moe(h, router_w, router_b, w1, b1, w2, b2):
    T, D = h.shape
    logits = (h @ router_w + router_b).astype(jnp.float32)
    top_val, top_idx = lax.top_k(logits, TOP_K)
    gate = jax.nn.softmax(top_val, axis=-1)
    flat_e = top_idx.reshape(-1)
    flat_tok = jnp.repeat(jnp.arange(T, dtype=jnp.int32), TOP_K)
    order = jnp.argsort(flat_e)
    e_sorted = flat_e[order]
    counts = jnp.bincount(flat_e, length=N_EXPERTS)
    padded = (counts + MOE_BLOCK - 1) // MOE_BLOCK * MOE_BLOCK
    start = jnp.cumsum(counts) - counts
    pad_end = jnp.cumsum(padded)
    pad_start = pad_end - padded
    dest = pad_start[e_sorted] + jnp.arange(T * TOP_K) - start[e_sorted]
    n_blocks = -(-(T * TOP_K + N_EXPERTS * (MOE_BLOCK - 1)) // MOE_BLOCK)
    n_slots = n_blocks * MOE_BLOCK
    slot_tok = jnp.full((n_slots,), T, jnp.int32).at[dest].set(flat_tok[order])
    slot_w = jnp.zeros((n_slots,), jnp.float32).at[dest].set(gate.reshape(-1)[order])
    block_e = jnp.minimum(jnp.searchsorted(pad_end, jnp.arange(n_blocks) * MOE_BLOCK, side='right'),
                          N_EXPERTS - 1).astype(jnp.int32)
    h_pad = jnp.concatenate([h, jnp.zeros((1, D), h.dtype)], 0)

    def expert_block(acc, blk):
        tok, wt, e = blk
        w1e = lax.dynamic_index_in_dim(w1, e, keepdims=False)
        b1e = lax.dynamic_index_in_dim(b1, e, keepdims=False)
        w2e = lax.dynamic_index_in_dim(w2, e, keepdims=False)
        b2e = lax.dynamic_index_in_dim(b2, e, keepdims=False)
        hid = h_pad[tok] @ w1e + b1e
        glu, lin = jnp.split(hid, 2, axis=-1)
        glu = jnp.minimum(glu, SWIGLU_LIMIT)
        lin = jnp.clip(lin, -SWIGLU_LIMIT, SWIGLU_LIMIT)
        act = glu * jax.nn.sigmoid(SWIGLU_ALPHA * glu) * (lin + 1)
        out = (act @ w2e + b2e) * wt[:, None].astype(h.dtype)
        return acc.at[tok].add(out), None

    acc, _ = lax.scan(expert_block, jnp.zeros((T + 1, D), h.dtype),
                      (slot_tok.reshape(n_blocks, MOE_BLOCK), slot_w.reshape(n_blocks, MOE_BLOCK), block_e))
    return acc[:T]


def setup_inputs(seed: int = 0) -> dict:
    key = jax.random.key(seed)
    ks = iter(jax.random.split(key, 64))
    L = DEPTH

    def nrm(shape, scale):
        return jax.random.normal(next(ks), shape, jnp.float32) * scale

    def unif(shape):
        return jax.random.uniform(next(ks), shape, jnp.float32)

    x = nrm((BATCH, SEQ, D_MODEL), 1.0)
    c = nrm((BATCH, D_MODEL), 1.0)
    offsets = jax.random.randint(next(ks), (BATCH, 1), 0, MAX_STREAM_OFFSET, dtype=jnp.int32)
    positions = offsets + jnp.arange(SEQ, dtype=jnp.int32)[None, :]
    return {
        'x': x, 'c': c, 'positions': positions,
        'mod_w': nrm((D_MODEL, N_MOD * D_MODEL), 0.2 * D_MODEL ** -0.5),
        'mod_b': nrm((N_MOD * D_MODEL,), 0.02),
        'mod_offset': nrm((L, N_MOD * D_MODEL), 0.02),
        'ln_post_g': 1.0 + nrm((L, 2, D_MODEL), 0.02),
        'ln_post_b': nrm((L, 2, D_MODEL), 0.02),
        'w_in': nrm((L, D_MODEL, D_IN), D_MODEL ** -0.5),
        'w_out': nrm((L, D_MIX, D_MODEL), DN_BETA * D_MIX ** -0.5),
        'conv_dw': nrm((L, CONV_WIDTH, GROUP), CONV_WIDTH ** -0.5),
        'conv_b': nrm((L, GROUP), 0.02),
        'conv_ln_g': 1.0 + nrm((L, GROUP), 0.02),
        'conv_ln_b': nrm((L, GROUP), 0.02),
        'mla_q_norm': 1.0 + nrm((L, MLA_Q_RANK), 0.02),
        'mla_kv_norm': 1.0 + nrm((L, MLA_KV_RANK), 0.02),
        'mla_w_uq': nrm((L, MLA_Q_RANK, MLA_HEADS * (MLA_NOPE + MLA_ROPE)), MLA_Q_RANK ** -0.5),
        'mla_w_ukv': nrm((L, MLA_KV_RANK, MLA_HEADS * (MLA_NOPE + MLA_VDIM)), MLA_KV_RANK ** -0.5),
        'rw_mu_rkv': unif((L, 3, GROUP)),
        'rw_mu_x': unif((L, 3, D_MODEL)),
        'rw_w0': nrm((L, GROUP), 0.5),
        'rw_w1': nrm((L, D_MODEL, RW_DECAY_LORA), D_MODEL ** -0.5),
        'rw_w2': nrm((L, RW_DECAY_LORA, GROUP), 0.5 * RW_DECAY_LORA ** -0.5),
        'rw_a0': nrm((L, GROUP), 0.1),
        'rw_a1': nrm((L, D_MODEL, RW_AAA_LORA), D_MODEL ** -0.5),
        'rw_a2': nrm((L, RW_AAA_LORA, GROUP), 0.5 * RW_AAA_LORA ** -0.5),
        'rw_g1': nrm((L, D_MODEL, RW_GATE_LORA), D_MODEL ** -0.5),
        'rw_g2': nrm((L, RW_GATE_LORA, GROUP), RW_GATE_LORA ** -0.5),
        'rw_k_k': 0.85 + nrm((L, GROUP), 0.02),
        'rw_k_a': 1.0 + nrm((L, GROUP), 0.02),
        'rw_r_k': nrm((L, RW_HEADS, RW_HEAD), 0.1),
        'rw_gn_g': 1.0 + nrm((L, GROUP), 0.02),
        'rw_gn_b': nrm((L, GROUP), 0.02),
        'rw_vres_mu': unif((L - 1, D_MODEL)),
        'rw_v0': nrm((L - 1, GROUP), 0.1),
        'rw_v1': nrm((L - 1, D_MODEL, RW_MV_LORA), D_MODEL ** -0.5),
        'rw_v2': nrm((L - 1, RW_MV_LORA, GROUP), 0.5 * RW_MV_LORA ** -0.5),
        'ca_rel_bias': nrm((L, CA_HEADS, 2 * REL_CLIP + 1), 0.1),
        'moe_router_w': nrm((L, D_MODEL, N_EXPERTS), D_MODEL ** -0.5),
        'moe_router_b': nrm((L, N_EXPERTS), 0.01),
        'moe_w1': nrm((L, N_EXPERTS, D_MODEL, 2 * D_FF), D_MODEL ** -0.5),
        'moe_b1': nrm((L, N_EXPERTS, 2 * D_FF), 0.02),
        'moe_w2': nrm((L, N_EXPERTS, D_FF, D_MODEL), DN_BETA * D_FF ** -0.5),
        'moe_b2': nrm((L, N_EXPERTS, D_MODEL), 0.02),
    }


def reference(x, c, positions, mod_w, mod_b, mod_offset, ln_post_g, ln_post_b, w_in, w_out,
              conv_dw, conv_b, conv_ln_g, conv_ln_b, mla_q_norm, mla_kv_norm, mla_w_uq, mla_w_ukv,
              rw_mu_rkv, rw_mu_x, rw_w0, rw_w1, rw_w2, rw_a0, rw_a1, rw_a2, rw_g1, rw_g2,
              rw_k_k, rw_k_a, rw_r_k, rw_gn_g, rw_gn_b, rw_vres_mu, rw_v0, rw_v1, rw_v2,
              ca_rel_bias, moe_router_w, moe_router_b, moe_w1, moe_b1, moe_w2, moe_b2):
    B, S, D = x.shape
    cos, sin = rope_tables(positions)
    mod_shared = jax.nn.silu(c) @ mod_w + mod_b
    v_first = None
    for l in range(DEPTH):
        mod = mod_shared + mod_offset[l]
        sh1, sc1, g1, sh2, sc2, g2 = [m[:, None, :] for m in jnp.split(mod, N_MOD, axis=-1)]

        h = layer_norm(x) * (1 + sc1) + sh1
        u_conv, cq, ckv, kpe, rkv, qkv = jnp.split(h @ w_in[l], IN_OFFSETS, axis=-1)
        y_conv = conv_module(u_conv, conv_dw[l], conv_b[l], conv_ln_g[l], conv_ln_b[l])
        y_mla = mla(cq, ckv, kpe, mla_q_norm[l], mla_kv_norm[l], mla_w_uq[l], mla_w_ukv[l], cos, sin)
        r, k, v = jnp.split(rkv, 3, axis=-1)
        vres = None if l == 0 else (rw_vres_mu[l - 1], rw_v0[l - 1], rw_v1[l - 1], rw_v2[l - 1])
        y_rw, v_first = rwkv7_time_mix(h, r, k, v, v_first, rw_mu_rkv[l], rw_mu_x[l], rw_w0[l],
                                       rw_w1[l], rw_w2[l], rw_a0[l], rw_a1[l], rw_a2[l],
                                       rw_g1[l], rw_g2[l], rw_k_k[l], rw_k_a[l], rw_r_k[l],
                                       rw_gn_g[l], rw_gn_b[l], vres)
        qa, ka, va = jnp.split(qkv, 3, axis=-1)
        y_ca = chunk_band_attention(qa, ka, va, ca_rel_bias[l])
        y = jnp.concatenate([y_conv, y_mla, y_rw, y_ca], axis=-1) @ w_out[l]
        x = layer_norm(DN_ALPHA * x + (1 + g1) * y, ln_post_g[l, 0], ln_post_b[l, 0])

        h = layer_norm(x) * (1 + sc2) + sh2
        y = moe(h.reshape(B * S, D), moe_router_w[l], moe_router_b[l], moe_w1[l], moe_b1[l],
                moe_w2[l], moe_b2[l]).reshape(B, S, D)
        x = layer_norm(DN_ALPHA * x + (1 + g2) * y, ln_post_g[l, 1], ln_post_b[l, 1])
    return x
```

```python
import functools

import numpy as np
import jax
import jax.numpy as jnp
from jax import lax
from jax.experimental import pallas as pl
from jax.experimental.pallas import tpu as pltpu

F32 = jnp.float32
BF16 = jnp.bfloat16
HIGHEST = lax.Precision.HIGHEST

DEPTH = 2
CHUNK = 64
LN_EPS = 1e-5
DN_ALPHA = (2 * DEPTH) ** 0.25
N_MOD = 6
CONV_WIDTH = 31
CONV_HALO = 32
MLA_HEADS = 4
MLA_NOPE = 128
MLA_ROPE = 64
MLA_VDIM = 128
MLA_QK_PAD = 256
MLA_Q_RANK = 384
MLA_KV_RANK = 256
ROPE_THETA = 10000.0
RW_HEAD = 64
RW_GN_EPS = 64e-5
RW_GROUP_HEADS = 4
CA_HEADS = 4
CA_LEFT = 8
REL_CLIP = 256
N_EXPERTS = 32
TOP_K = 4
SWIGLU_LIMIT = 7.0
SWIGLU_ALPHA = 1.702
LORA_PAD = (128, 128, 256, 128)
V7X_VMEM_LIMIT = 56 * 1024 * 1024

TM_PROJ = 256
TM_CONV = 256
TM_PREP = 512
TQ_MLA = 256
TM_BAND = 256
TM_OUT = 256
TM_MOE = 512
TF_MOE = 256
TM_FIN = 256


def _cparams(sem, vmem=V7X_VMEM_LIMIT):
    return pltpu.CompilerParams(dimension_semantics=sem, vmem_limit_bytes=vmem)


def _ln_rows(x):
    mu = jnp.mean(x, axis=-1, keepdims=True)
    xc = x - mu
    var = jnp.mean(xc * xc, axis=-1, keepdims=True)
    return xc * lax.rsqrt(var + LN_EPS)


def _sigmoid(x):
    return 1.0 / (1.0 + jnp.exp(-x))


def _in_proj_kernel(x_ref, sc_ref, sh_ref, w_ref, *out_refs, widths):
    h = _ln_rows(x_ref[0]) * (1.0 + sc_ref[0]) + sh_ref[0]
    hb = h.astype(BF16)
    off = 0
    for o_ref, wd in zip(out_refs, widths):
        o_ref[0] = jnp.dot(hb, w_ref[:, off:off + wd], preferred_element_type=F32).astype(o_ref.dtype)
        off += wd


def _in_proj(x, sc, sh, w_big, widths):
    B, S, D = x.shape
    tm = TM_PROJ
    return pl.pallas_call(
        functools.partial(_in_proj_kernel, widths=widths),
        grid=(B, S // tm),
        in_specs=[
            pl.BlockSpec((1, tm, D), lambda b, i: (b, i, 0)),
            pl.BlockSpec((1, 1, D), lambda b, i: (b, 0, 0)),
            pl.BlockSpec((1, 1, D), lambda b, i: (b, 0, 0)),
            pl.BlockSpec(memory_space=pltpu.VMEM),
        ],
        out_specs=[pl.BlockSpec((1, tm, wd), lambda b, i: (b, i, 0)) for wd in widths],
        out_shape=[jax.ShapeDtypeStruct((B, S, wd), F32) for wd in widths],
        compiler_params=_cparams(("parallel", "parallel")),
    )(x, sc, sh, w_big)


def _conv_kernel(u_ref, halo_ref, dw_ref, db_ref, g_ref, b_ref, o_ref, scr, *, rows):
    G = o_ref.shape[-1]
    tm = o_ref.shape[1]
    i = pl.program_id(1)

    def glu(u):
        return u[:, :G] * _sigmoid(u[:, G:])

    scr[0:CONV_HALO, :] = jnp.where(i > 0, glu(halo_ref[0]), 0.0)
    scr[CONV_HALO:, :] = glu(u_ref[0])
    lead = CONV_HALO - (CONV_WIDTH - 1)

    for r0 in range(0, tm, rows):
        acc = jnp.zeros((rows, G), F32) + db_ref[...]
        for j in range(CONV_WIDTH):
            acc = acc + dw_ref[j:j + 1, :] * scr[r0 + lead + j:r0 + lead + j + rows, :]
        y = _ln_rows(acc) * g_ref[...] + b_ref[...]
        o_ref[0, r0:r0 + rows, :] = (y * _sigmoid(y)).astype(o_ref.dtype)


def _conv_module(u2, dw, db, ln_g, ln_b):
    B, S, G2 = u2.shape
    G = G2 // 2
    tm = TM_CONV
    nh = tm // CONV_HALO
    return pl.pallas_call(
        functools.partial(_conv_kernel, rows=32),
        grid=(B, S // tm),
        in_specs=[
            pl.BlockSpec((1, tm, G2), lambda b, i: (b, i, 0)),
            pl.BlockSpec((1, CONV_HALO, G2), lambda b, i: (b, jnp.maximum(i * nh - 1, 0), 0)),
            pl.BlockSpec((CONV_WIDTH, G), lambda b, i: (0, 0)),
            pl.BlockSpec((1, G), lambda b, i: (0, 0)),
            pl.BlockSpec((1, G), lambda b, i: (0, 0)),
            pl.BlockSpec((1, G), lambda b, i: (0, 0)),
        ],
        out_specs=pl.BlockSpec((1, tm, G), lambda b, i: (b, i, 0)),
        out_shape=jax.ShapeDtypeStruct((B, S, G), BF16),
        scratch_shapes=[pltpu.VMEM((CONV_HALO + tm, G), F32)],
        compiler_params=_cparams(("parallel", "parallel")),
    )(u2, u2, dw, db.reshape(1, G), ln_g.reshape(1, G), ln_b.reshape(1, G))


def _mla_prep_kernel(m_ref, ct_ref, st_ref, qn_ref, kvn_ref, wqa_ref, wqr_ref, wkn_ref, wv_ref,
                     pa_ref, pr_ref, q_ref, k_ref, v_ref, *, scale):
    m = m_ref[0]
    cq = m[:, :MLA_Q_RANK]
    ckv = m[:, MLA_Q_RANK:MLA_Q_RANK + MLA_KV_RANK]
    kpe = m[:, MLA_Q_RANK + MLA_KV_RANK:MLA_Q_RANK + MLA_KV_RANK + 128]
    ct = ct_ref[0]
    st = st_ref[0]

    def rms(t, g):
        return t * lax.rsqrt(jnp.mean(t * t, axis=-1, keepdims=True) + 1e-6) * g

    cqn = rms(cq, qn_ref[...]).astype(BF16)
    ckvn = rms(ckv, kvn_ref[...]).astype(BF16)
    qa = jnp.dot(cqn, wqa_ref[...], preferred_element_type=F32)
    qr = jnp.dot(cqn, wqr_ref[...], preferred_element_type=F32)
    kn = jnp.dot(ckvn, wkn_ref[...], preferred_element_type=F32)
    v_ref[0] = jnp.dot(ckvn, wv_ref[...], preferred_element_type=F32).astype(v_ref.dtype)
    kp = (jnp.dot(kpe, pa_ref[...], precision=HIGHEST, preferred_element_type=F32) * ct
          + jnp.dot(kpe, pr_ref[...], precision=HIGHEST, preferred_element_type=F32) * st)
    W = MLA_QK_PAD
    for h in range(MLA_HEADS):
        sl = slice(h * W, (h + 1) * W)
        q_ref[0, :, sl] = ((qa[:, sl] * ct + qr[:, sl] * st) * scale).astype(q_ref.dtype)
        k_ref[0, :, sl] = (kn[:, sl] + kp).astype(k_ref.dtype)


def _mla_prep(mla_in, ctab, stab, q_norm, kv_norm, wqa, wqr, wkn, wv, pa, pr):
    B, S, Wm = mla_in.shape
    tm = TM_PREP
    HW = MLA_HEADS * MLA_QK_PAD
    HV = MLA_HEADS * MLA_VDIM
    full = lambda a: pl.BlockSpec(a.shape, lambda b, i: (0,) * a.ndim)
    qn = q_norm.reshape(1, -1)
    kvn = kv_norm.reshape(1, -1)
    scale = (MLA_NOPE + MLA_ROPE) ** -0.5
    return pl.pallas_call(
        functools.partial(_mla_prep_kernel, scale=scale),
        grid=(B, S // tm),
        in_specs=[
            pl.BlockSpec((1, tm, Wm), lambda b, i: (b, i, 0)),
            pl.BlockSpec((1, tm, MLA_QK_PAD), lambda b, i: (b, i, 0)),
            pl.BlockSpec((1, tm, MLA_QK_PAD), lambda b, i: (b, i, 0)),
            full(qn), full(kvn), full(wqa), full(wqr), full(wkn), full(wv), full(pa), full(pr),
        ],
        out_specs=[
            pl.BlockSpec((1, tm, HW), lambda b, i: (b, i, 0)),
            pl.BlockSpec((1, tm, HW), lambda b, i: (b, i, 0)),
            pl.BlockSpec((1, tm, HV), lambda b, i: (b, i, 0)),
        ],
        out_shape=[
            jax.ShapeDtypeStruct((B, S, HW), BF16),
            jax.ShapeDtypeStruct((B, S, HW), BF16),
            jax.ShapeDtypeStruct((B, S, HV), BF16),
        ],
        compiler_params=_cparams(("parallel", "parallel")),
    )(mla_in, ctab, stab, qn, kvn, wqa, wqr, wkn, wv, pa, pr)


def _mla_attn_kernel(q_ref, k_ref, v_ref, o_ref):
    tq = q_ref.shape[1]
    i = pl.program_id(2)
    q = q_ref[0]
    row_c = lax.broadcasted_iota(jnp.int32, (tq, tq), 0) // CHUNK
    col_c = lax.broadcasted_iota(jnp.int32, (tq, tq), 1) // CHUNK
    diag_ok = col_c <= row_c

    def step(j, carry, masked):
        m, l, acc = carry
        r0 = pl.multiple_of(j * tq, tq)
        k = k_ref[0, pl.ds(r0, tq), :]
        v = v_ref[0, pl.ds(r0, tq), :]
        s = lax.dot_general(q, k, (((1,), (1,)), ((), ())), preferred_element_type=F32)
        if masked:
            s = jnp.where(diag_ok, s, -jnp.inf)
        m_new = jnp.maximum(m, jnp.max(s, axis=-1, keepdims=True))
        alpha = jnp.exp(m - m_new)
        p = jnp.exp(s - m_new)
        l = alpha * l + jnp.sum(p, axis=-1, keepdims=True)
        acc = alpha * acc + jnp.dot(p.astype(BF16), v, preferred_element_type=F32)
        return m_new, l, acc

    init = (jnp.full((tq, 1), -jnp.inf, F32), jnp.zeros((tq, 1), F32), jnp.zeros((tq, MLA_VDIM), F32))
    carry = lax.fori_loop(0, i, lambda j, c: step(j, c, False), init)
    _, l, acc = step(i, carry, True)
    o_ref[0] = (acc / l).astype(o_ref.dtype)


def _mla_attn(q, k, v):
    B, S, _ = q.shape
    tq = TQ_MLA
    return pl.pallas_call(
        _mla_attn_kernel,
        grid=(B, MLA_HEADS, S // tq),
        in_specs=[
            pl.BlockSpec((1, tq, MLA_QK_PAD), lambda b, h, i: (b, i, h)),
            pl.BlockSpec((1, S, MLA_QK_PAD), lambda b, h, i: (b, 0, h)),
            pl.BlockSpec((1, S, MLA_VDIM), lambda b, h, i: (b, 0, h)),
        ],
        out_specs=pl.BlockSpec((1, tq, MLA_VDIM), lambda b, h, i: (b, i, h)),
        out_shape=jax.ShapeDtypeStruct((B, S, MLA_HEADS * MLA_VDIM), BF16),
        compiler_params=_cparams(("parallel", "parallel", "arbitrary")),
    )(q, k, v)


def _band_attn_kernel(q_ref, k_ref, v_ref, bias_ref, o_ref, *, scale):
    tm = q_ref.shape[1]
    hd = q_ref.shape[2] // CA_HEADS
    W = (CA_LEFT + 1) * CHUNK
    i = pl.program_id(1)
    wchunk = lax.broadcasted_iota(jnp.int32, (CHUNK, W), 1) // CHUNK
    for c in range(tm // CHUNK):
        n = i * (tm // CHUNK) + c
        r0 = pl.multiple_of(n * CHUNK, CHUNK)
        valid = wchunk >= CA_LEFT - n
        for h in range(CA_HEADS):
            q = q_ref[0, c * CHUNK:(c + 1) * CHUNK, h * hd:(h + 1) * hd]
            k = k_ref[0, pl.ds(r0, W), h * hd:(h + 1) * hd]
            v = v_ref[0, pl.ds(r0, W), h * hd:(h + 1) * hd]
            s = lax.dot_general(q, k, (((1,), (1,)), ((), ())), preferred_element_type=F32) * scale
            s = jnp.where(valid, s + bias_ref[h], -jnp.inf)
            m = jnp.max(s, axis=-1, keepdims=True)
            p = jnp.exp(s - m)
            l = jnp.sum(p, axis=-1, keepdims=True)
            o = jnp.dot(p.astype(BF16), v, preferred_element_type=F32) / l
            o_ref[0, c * CHUNK:(c + 1) * CHUNK, h * hd:(h + 1) * hd] = o.astype(o_ref.dtype)


def _band_attn(q, kp, vp, bias):
    B, S, G = q.shape
    tm = TM_BAND
    Sp = kp.shape[1]
    hd = G // CA_HEADS
    return pl.pallas_call(
        functools.partial(_band_attn_kernel, scale=hd ** -0.5),
        grid=(B, S // tm),
        in_specs=[
            pl.BlockSpec((1, tm, G), lambda b, i: (b, i, 0)),
            pl.BlockSpec((1, Sp, G), lambda b, i: (b, 0, 0)),
            pl.BlockSpec((1, Sp, G), lambda b, i: (b, 0, 0)),
            pl.BlockSpec(bias.shape, lambda b, i: (0, 0, 0)),
        ],
        out_specs=pl.BlockSpec((1, tm, G), lambda b, i: (b, i, 0)),
        out_shape=jax.ShapeDtypeStruct((B, S, G), BF16),
        compiler_params=_cparams(("parallel", "arbitrary")),
    )(q, kp, vp, bias)


def _head_sum_matrix(n, dtype=F32):
    r = lax.broadcasted_iota(jnp.int32, (n, n), 0) // RW_HEAD
    c = lax.broadcasted_iota(jnp.int32, (n, n), 1) // RW_HEAD
    return (r == c).astype(dtype)


def _rw_prep_kernel(*refs, has_vres):
    if has_vres:
        (rkv_ref, rkvh_ref, lo_ref, loh_ref, mu_ref, w0_ref, w2_ref, a0_ref, a2_ref, g2_ref, kk_ref, ka_ref,
         vf_ref, v0_ref, v2_ref, r_o, k_o, v_o, kk_o, a_o, lw_o, g_o) = refs
    else:
        (rkv_ref, rkvh_ref, lo_ref, loh_ref, mu_ref, w0_ref, w2_ref, a0_ref, a2_ref, g2_ref, kk_ref, ka_ref,
         r_o, k_o, v_o, kk_o, a_o, lw_o, g_o) = refs
    G = r_o.shape[-1]
    tm = r_o.shape[1]
    i = pl.program_id(1)
    first = lax.broadcasted_iota(jnp.int32, (tm, 1), 0) == 0

    def shifted(cur, halo):
        prev = jnp.where(i > 0, halo[7:8, :], 0.0)
        return jnp.where(first, prev, pltpu.roll(cur, 1, axis=0))

    rkv = rkv_ref[0]
    rkv = rkv + (shifted(rkv, rkvh_ref[0]) - rkv) * mu_ref[...]
    r = rkv[:, :G]
    k = rkv[:, G:2 * G]
    v = rkv[:, 2 * G:]

    lo = lo_ref[0]
    LP = sum(LORA_PAD)
    lin = lo[:, :LP] + shifted(lo[:, LP:], loh_ref[0][:, LP:])
    o1 = LORA_PAD[0]
    o2 = o1 + LORA_PAD[1]
    o3 = o2 + LORA_PAD[2]
    w_raw = w0_ref[...] + jnp.dot(jnp.tanh(lin[:, :o1]).astype(BF16), w2_ref[...], preferred_element_type=F32)
    w = -(jnp.maximum(-w_raw, 0.0) + jnp.log(1.0 + jnp.exp(-jnp.abs(w_raw)))) - 0.5
    lw_o[0] = -jnp.exp(w)
    a = _sigmoid(a0_ref[...] + jnp.dot(lin[:, o1:o2].astype(BF16), a2_ref[...], preferred_element_type=F32))
    g_o[0] = jnp.dot(_sigmoid(lin[:, o2:o3]).astype(BF16), g2_ref[...], preferred_element_type=F32)
    kk = k * kk_ref[...]
    n2 = jnp.dot(kk * kk, _head_sum_matrix(G), precision=HIGHEST, preferred_element_type=F32)
    kk_o[0] = kk / jnp.maximum(jnp.sqrt(n2), 1e-12)
    k_o[0] = k * (1.0 + (a - 1.0) * ka_ref[...])
    if has_vres:
        mix = _sigmoid(v0_ref[...] + jnp.dot(lin[:, o3:].astype(BF16), v2_ref[...], preferred_element_type=F32))
        v = v + (vf_ref[0] - v) * mix
    r_o[0] = r
    v_o[0] = v
    a_o[0] = a


def _rw_prep(rkv, lora, mu_rkv, w0, w2p, a0, a2p, g2, k_k, k_a, vres):
    B, S, G3 = rkv.shape
    G = G3 // 3
    LW = lora.shape[-1]
    tm = TM_PREP
    nh = tm // 8
    row = lambda a: a.reshape(1, -1)
    full = lambda a: pl.BlockSpec(a.shape, lambda b, i: (0,) * a.ndim)
    halo = lambda w: pl.BlockSpec((1, 8, w), lambda b, i: (b, jnp.maximum(i * nh - 1, 0), 0))
    tile = lambda w: pl.BlockSpec((1, tm, w), lambda b, i: (b, i, 0))
    args = [rkv, rkv, lora, lora, row(mu_rkv), row(w0), w2p, row(a0), a2p, g2, row(k_k), row(k_a)]
    specs = [tile(G3), halo(G3), tile(LW), halo(LW)] + [full(a) for a in args[4:]]
    if vres is not None:
        v_first, v0, v2p = vres
        extra = [v_first, row(v0), v2p]
        args += extra
        specs += [tile(G), full(extra[1]), full(extra[2])]
    return pl.pallas_call(
        functools.partial(_rw_prep_kernel, has_vres=vres is not None),
        grid=(B, S // tm),
        in_specs=specs,
        out_specs=[tile(G)] * 7,
        out_shape=[jax.ShapeDtypeStruct((B, S, G), F32)] * 7,
        compiler_params=_cparams(("parallel", "parallel")),
    )(*args)


def _rw_group(r, k, v, kk, a, lw, st0, prec):
    C, N = r.shape
    H = N // RW_HEAD
    mm = lambda x, y: jnp.dot(x, y, precision=prec, preferred_element_type=F32)
    nt = lambda x, y: lax.dot_general(x, y, (((1,), (1,)), ((), ())), precision=prec, preferred_element_type=F32)
    tn = lambda x, y: lax.dot_general(x, y, (((0,), (0,)), ((), ())), precision=prec, preferred_element_type=F32)

    tr = lax.broadcasted_iota(jnp.int32, (C, C), 0)
    tc = lax.broadcasted_iota(jnp.int32, (C, C), 1)
    cum = jnp.dot((tc <= tr).astype(F32), lw, precision=HIGHEST, preferred_element_type=F32)
    tot = cum[C - 1:C, :]
    e_pos = jnp.exp(cum)
    e_neg = jnp.exp(-cum)
    e_end = jnp.exp(tot - cum)
    b = kk * a
    Aa = jnp.exp(cum - lw) * kk
    Bb = e_neg * b
    Kk = e_neg * k
    Rr = e_pos * r
    Bh = e_end * b
    Kh = e_end * k

    sr = lax.broadcasted_iota(jnp.int32, (H * C, N), 0)
    sc = lax.broadcasted_iota(jnp.int32, (H * C, N), 1)
    hmask = (sr // C) == (sc // RW_HEAD)
    tile = lambda x: jnp.concatenate([x] * H, axis=0)
    stack = lambda x: jnp.where(hmask, tile(x), 0.0)
    collapse = lambda xs: sum(xs[h * C:(h + 1) * C, :] for h in range(H))
    As, Rs, Vs = stack(Aa), stack(Rr), stack(v)
    Bt, Kt = tile(Bb), tile(Kk)

    br = lax.broadcasted_iota(jnp.int32, (H * C, H * C), 0)
    bc = lax.broadcasted_iota(jnp.int32, (H * C, H * C), 1)
    same = (br // C) == (bc // C)
    strict = same & ((br % C) > (bc % C))
    incl = same & ((br % C) >= (bc % C))
    eye = br == bc
    Lab = jnp.where(strict, nt(As, Bt), 0.0)
    Lak = jnp.where(strict, nt(As, Kt), 0.0)
    Mrb = jnp.where(incl, nt(Rs, Bt), 0.0)
    Mrk = jnp.where(incl, nt(Rs, Kt), 0.0)

    M = -Lab
    X = jnp.where(eye, 1.0, M)
    P = mm(M, M)
    steps = int(np.log2(C)) - 1
    for s in range(steps):
        X = X + mm(X, P)
        if s + 1 < steps:
            P = mm(P, P)
    W1s = mm(X, As)
    W2s = mm(X, mm(Lak, Vs))
    Q = collapse(Rs - mm(Mrb, W1s))
    Z = collapse(mm(Mrk, Vs) - mm(Mrb, W2s))
    W1 = collapse(W1s)
    W2 = collapse(W2s)
    kr = lax.broadcasted_iota(jnp.int32, (N, N), 0)
    kc = lax.broadcasted_iota(jnp.int32, (N, N), 1)
    bd = (kr // RW_HEAD) == (kc // RW_HEAD)
    Gm = jnp.where(kr == kc, jnp.exp(tot), 0.0) - jnp.where(bd, tn(Bh, W1), 0.0)
    Hm = jnp.where(bd, tn(Kh, v) - tn(Bh, W2), 0.0)
    y = mm(Q, st0) + Z
    st1 = mm(Gm, st0) + Hm
    return y, st1


def _rw_scan_kernel(r_ref, k_ref, v_ref, kk_ref, a_ref, lw_ref, g_ref, rk_ref, gg_ref, gb_ref, o_ref, st_ref, *, prec):
    n = pl.program_id(1)
    N = RW_GROUP_HEADS * RW_HEAD

    @pl.when(n == 0)
    def _():
        st_ref[...] = jnp.zeros_like(st_ref)

    hs = _head_sum_matrix(N)
    for gi in range(r_ref.shape[-1] // N):
        sl = slice(gi * N, (gi + 1) * N)
        r, k, v = r_ref[0, :, sl], k_ref[0, :, sl], v_ref[0, :, sl]
        y, st1 = _rw_group(r, k, v, kk_ref[0, :, sl], a_ref[0, :, sl], lw_ref[0, :, sl], st_ref[gi], prec)
        st_ref[gi] = st1
        hmean = lambda t: jnp.dot(t, hs, precision=HIGHEST, preferred_element_type=F32) * (1.0 / RW_HEAD)
        yc = y - hmean(y)
        yn = yc * lax.rsqrt(hmean(yc * yc) + RW_GN_EPS) * gg_ref[:, sl] + gb_ref[:, sl]
        bonus = jnp.dot(r * k * rk_ref[:, sl], hs, precision=HIGHEST, preferred_element_type=F32) * v
        o_ref[0, :, sl] = ((yn + bonus) * g_ref[0, :, sl]).astype(o_ref.dtype)


def _rw_scan(r, k, v, kk, a, lw, g, r_k, gn_g, gn_b, prec):
    B, S, G = r.shape
    N = RW_GROUP_HEADS * RW_HEAD
    tile = pl.BlockSpec((1, CHUNK, G), lambda b, n: (b, n, 0))
    rowspec = pl.BlockSpec((1, G), lambda b, n: (0, 0))
    return pl.pallas_call(
        functools.partial(_rw_scan_kernel, prec=prec),
        grid=(B, S // CHUNK),
        in_specs=[tile] * 7 + [rowspec] * 3,
        out_specs=tile,
        out_shape=jax.ShapeDtypeStruct((B, S, G), BF16),
        scratch_shapes=[pltpu.VMEM((G // N, N, N), F32)],
        compiler_params=_cparams(("parallel", "arbitrary")),
    )(r, k, v, kk, a, lw, g, r_k.reshape(1, G), gn_g.reshape(1, G), gn_b.reshape(1, G))


def _out_proj_kernel(yc_ref, ym_ref, yr_ref, ya_ref, w_ref, x_ref, g1_ref, lg_ref, lb_ref, sc_ref, sh_ref,
                     rw_ref, rb_ref, x1_ref, h2_ref, lo_ref):
    G = yc_ref.shape[-1]
    y = jnp.zeros(x_ref.shape[1:], F32)
    for j, y_ref in enumerate((yc_ref, ym_ref, yr_ref, ya_ref)):
        y = y + jnp.dot(y_ref[0], w_ref[j * G:(j + 1) * G, :], preferred_element_type=F32)
    x1 = _ln_rows(DN_ALPHA * x_ref[0] + (1.0 + g1_ref[0]) * y) * lg_ref[...] + lb_ref[...]
    x1_ref[0] = x1
    h2 = _ln_rows(x1) * (1.0 + sc_ref[0]) + sh_ref[0]
    h2_ref[0] = h2.astype(h2_ref.dtype)
    lo_ref[0] = jnp.dot(h2, rw_ref[...], precision=HIGHEST, preferred_element_type=F32) + rb_ref[...]


def _out_proj(ys, w_out, x, g1, ln_g, ln_b, sc2, sh2, router_w, router_b):
    B, S, D = x.shape
    G = ys[0].shape[-1]
    E = router_w.shape[-1]
    tm = TM_OUT
    tile = lambda w: pl.BlockSpec((1, tm, w), lambda b, i: (b, i, 0))
    mod = pl.BlockSpec((1, 1, D), lambda b, i: (b, 0, 0))
    full = lambda a: pl.BlockSpec(a.shape, lambda b, i: (0,) * a.ndim)
    lg, lb, rb = ln_g.reshape(1, D), ln_b.reshape(1, D), router_b.reshape(1, E)
    return pl.pallas_call(
        _out_proj_kernel,
        grid=(B, S // tm),
        in_specs=[tile(G)] * 4 + [pl.BlockSpec(memory_space=pltpu.VMEM), tile(D), mod, full(lg), full(lb), mod, mod,
                                  full(router_w), full(rb)],
        out_specs=[tile(D), tile(D), tile(E)],
        out_shape=[jax.ShapeDtypeStruct((B, S, D), F32), jax.ShapeDtypeStruct((B, S, D), BF16),
                   jax.ShapeDtypeStruct((B, S, E), F32)],
        compiler_params=_cparams(("parallel", "parallel")),
    )(*ys, w_out, x, g1, lg, lb, sc2, sh2, router_w, rb)


def _moe_kernel(te_ref, nt_ref, x_ref, wg_ref, wl_ref, bg_ref, bl_ref, w2_ref, b2_ref, sw_ref, o_ref, acc_ref):
    i = pl.program_id(0)
    f = pl.program_id(1)
    nf = pl.num_programs(1)
    live = i < nt_ref[0]

    @pl.when(f == 0)
    def _():
        acc_ref[...] = jnp.zeros_like(acc_ref)

    @pl.when(live)
    def _():
        x = x_ref[...]
        glu = jnp.dot(x, wg_ref[0].astype(BF16), preferred_element_type=F32) + bg_ref[0]
        lin = jnp.dot(x, wl_ref[0].astype(BF16), preferred_element_type=F32) + bl_ref[0]
        glu = jnp.minimum(glu, SWIGLU_LIMIT)
        lin = jnp.clip(lin, -SWIGLU_LIMIT, SWIGLU_LIMIT)
        act = glu * _sigmoid(SWIGLU_ALPHA * glu) * (lin + 1.0)
        acc_ref[...] += jnp.dot(act.astype(BF16), w2_ref[0].astype(BF16), preferred_element_type=F32)

    @pl.when(f == nf - 1)
    def _():
        o_ref[...] = ((acc_ref[...] + b2_ref[0]) * sw_ref[...]).astype(o_ref.dtype)


def _moe_experts(tile_e, n_live, xs, w1, b1, w2, b2, slot_w):
    n_slots, D = xs.shape
    E, _, F2 = w1.shape
    F = F2 // 2
    tm, tf = TM_MOE, TF_MOE
    nf = F // tf
    b1r = b1.reshape(E, 1, F2)
    b2r = b2.reshape(E, 1, D)
    fidx = lambda i, f, nt: jnp.where(i < nt[0], f, nf - 1)
    grid_spec = pltpu.PrefetchScalarGridSpec(
        num_scalar_prefetch=2,
        grid=(n_slots // tm, nf),
        in_specs=[
            pl.BlockSpec((tm, D), lambda i, f, te, nt: (i, 0)),
            pl.BlockSpec((1, D, tf), lambda i, f, te, nt: (te[i], 0, fidx(i, f, nt))),
            pl.BlockSpec((1, D, tf), lambda i, f, te, nt: (te[i], 0, nf + fidx(i, f, nt))),
            pl.BlockSpec((1, 1, tf), lambda i, f, te, nt: (te[i], 0, fidx(i, f, nt))),
            pl.BlockSpec((1, 1, tf), lambda i, f, te, nt: (te[i], 0, nf + fidx(i, f, nt))),
            pl.BlockSpec((1, tf, D), lambda i, f, te, nt: (te[i], fidx(i, f, nt), 0)),
            pl.BlockSpec((1, 1, D), lambda i, f, te, nt: (te[i], 0, 0)),
            pl.BlockSpec((tm, 1), lambda i, f, te, nt: (i, 0)),
        ],
        out_specs=pl.BlockSpec((tm, D), lambda i, f, te, nt: (i, 0)),
        scratch_shapes=[pltpu.VMEM((tm, D), F32)],
    )
    return pl.pallas_call(
        _moe_kernel,
        grid_spec=grid_spec,
        out_shape=jax.ShapeDtypeStruct((n_slots, D), F32),
        compiler_params=_cparams(("arbitrary", "arbitrary")),
    )(tile_e, n_live, xs, w1, w1, b1r, b1r, w2, b2r, slot_w.reshape(n_slots, 1))


def _final_kernel(x_ref, y_ref, g2_ref, lg_ref, lb_ref, o_ref):
    D = x_ref.shape[-1]
    y = sum(y_ref[0, :, j * D:(j + 1) * D] for j in range(TOP_K))
    o_ref[0] = _ln_rows(DN_ALPHA * x_ref[0] + (1.0 + g2_ref[0]) * y) * lg_ref[...] + lb_ref[...]


def _final(x1, y4, g2, ln_g, ln_b):
    B, S, D = x1.shape
    tm = TM_FIN
    return pl.pallas_call(
        _final_kernel,
        grid=(B, S // tm),
        in_specs=[
            pl.BlockSpec((1, tm, D), lambda b, i: (b, i, 0)),
            pl.BlockSpec((1, tm, TOP_K * D), lambda b, i: (b, i, 0)),
            pl.BlockSpec((1, 1, D), lambda b, i: (b, 0, 0)),
            pl.BlockSpec((1, D), lambda b, i: (0, 0)),
            pl.BlockSpec((1, D), lambda b, i: (0, 0)),
        ],
        out_specs=pl.BlockSpec((1, tm, D), lambda b, i: (b, i, 0)),
        out_shape=jax.ShapeDtypeStruct((B, S, D), F32),
        compiler_params=_cparams(("parallel", "parallel")),
    )(x1, y4, g2, ln_g.reshape(1, D), ln_b.reshape(1, D))


def _pad_cols(w, n):
    return jnp.pad(w, ((0, 0), (0, n - w.shape[1])))


def _pad_rows(w, n):
    return jnp.pad(w, ((0, n - w.shape[0]), (0, 0)))


def _lora_fold(mu, w1, n):
    return _pad_cols((1.0 - mu)[:, None] * w1, n), _pad_cols(mu[:, None] * w1, n)


def _mla_weights(w_uq, w_ukv):
    Rq = w_uq.shape[0]
    W = MLA_QK_PAD
    half = MLA_ROPE // 2
    uq = w_uq.reshape(Rq, MLA_HEADS, MLA_NOPE + MLA_ROPE)
    pe = uq[:, :, MLA_NOPE:]
    pe_rot = jnp.concatenate([-pe[:, :, half:], pe[:, :, :half]], axis=-1)
    zpad = jnp.zeros((Rq, MLA_HEADS, W - MLA_NOPE - MLA_ROPE), F32)
    wqa = jnp.concatenate([uq, zpad], axis=-1).reshape(Rq, MLA_HEADS * W)
    wqr = jnp.concatenate([jnp.zeros_like(uq[:, :, :MLA_NOPE]), pe_rot, zpad], axis=-1).reshape(Rq, MLA_HEADS * W)
    Rk = w_ukv.shape[0]
    ukv = w_ukv.reshape(Rk, MLA_HEADS, MLA_NOPE + MLA_VDIM)
    wkn = jnp.concatenate([ukv[:, :, :MLA_NOPE], jnp.zeros((Rk, MLA_HEADS, W - MLA_NOPE), F32)], axis=-1)
    wkn = wkn.reshape(Rk, MLA_HEADS * W)
    wv = ukv[:, :, MLA_NOPE:].reshape(Rk, MLA_HEADS * MLA_VDIM)
    pa = np.zeros((128, W), np.float32)
    pr = np.zeros((128, W), np.float32)
    for d in range(MLA_ROPE):
        pa[d, MLA_NOPE + d] = 1.0
    for d in range(half):
        pr[half + d, MLA_NOPE + d] = -1.0
        pr[d, MLA_NOPE + half + d] = 1.0
    return wqa.astype(BF16), wqr.astype(BF16), wkn.astype(BF16), wv.astype(BF16), jnp.asarray(pa), jnp.asarray(pr)


def _rope_tables(positions):
    B, S = positions.shape
    inv = ROPE_THETA ** (-jnp.arange(0, MLA_ROPE, 2, dtype=F32) / MLA_ROPE)
    ang = positions.astype(F32)[..., None] * inv
    cos, sin = jnp.cos(ang), jnp.sin(ang)
    one = jnp.ones((B, S, MLA_NOPE), F32)
    zero = jnp.zeros((B, S, MLA_NOPE), F32)
    zpad = jnp.zeros((B, S, MLA_QK_PAD - MLA_NOPE - MLA_ROPE), F32)
    return (jnp.concatenate([one, cos, cos, zpad], -1), jnp.concatenate([zero, sin, sin, zpad], -1))


def _band_bias(rel_table):
    W = (CA_LEFT + 1) * CHUNK
    q_pos = CA_LEFT * CHUNK + jnp.arange(CHUNK)[:, None]
    k_pos = jnp.arange(W)[None, :]
    rel = jnp.clip(q_pos - k_pos, -REL_CLIP, REL_CLIP) + REL_CLIP
    return rel_table[:, rel].astype(F32)


def _route(logits, tm):
    T = logits.shape[0]
    top_val, top_idx = lax.top_k(logits, TOP_K)
    gate = jax.nn.softmax(top_val, axis=-1)
    flat_e = top_idx.reshape(-1)
    flat_tok = jnp.repeat(jnp.arange(T, dtype=jnp.int32), TOP_K)
    order = jnp.argsort(flat_e)
    e_sorted = flat_e[order]
    counts = jnp.bincount(flat_e, length=N_EXPERTS)
    padded = (counts + tm - 1) // tm * tm
    start = jnp.cumsum(counts) - counts
    pad_end = jnp.cumsum(padded)
    pad_start = pad_end - padded
    dest = (pad_start[e_sorted] + jnp.arange(T * TOP_K) - start[e_sorted]).astype(jnp.int32)
    n_tiles = -(-(T * TOP_K + N_EXPERTS * (tm - 1)) // tm)
    n_slots = n_tiles * tm
    slot_tok = jnp.zeros((n_slots,), jnp.int32).at[dest].set(flat_tok[order])
    slot_w = jnp.zeros((n_slots,), F32).at[dest].set(gate.reshape(-1)[order])
    pos = jnp.zeros((T * TOP_K,), jnp.int32).at[order].set(dest)
    n_live = (pad_end[-1] // tm).astype(jnp.int32)
    tile_e = jnp.minimum(jnp.searchsorted(pad_end, jnp.arange(n_tiles) * tm, side='right'), N_EXPERTS - 1)
    last_e = tile_e[jnp.maximum(n_live - 1, 0)]
    tile_e = jnp.where(jnp.arange(n_tiles) < n_live, tile_e, last_e).astype(jnp.int32)
    return slot_tok, slot_w, pos, tile_e, n_live.reshape(1)


def kernel(x, c, positions, mod_w, mod_b, mod_offset, ln_post_g, ln_post_b, w_in, w_out, conv_dw, conv_b, conv_ln_g, conv_ln_b, mla_q_norm, mla_kv_norm, mla_w_uq, mla_w_ukv, rw_mu_rkv, rw_mu_x, rw_w0, rw_w1, rw_w2, rw_a0, rw_a1, rw_a2, rw_g1, rw_g2, rw_k_k, rw_k_a, rw_r_k, rw_gn_g, rw_gn_b, rw_vres_mu, rw_v0, rw_v1, rw_v2, ca_rel_bias, moe_router_w, moe_router_b, moe_w1, moe_b1, moe_w2, moe_b2):
    B, S, D = x.shape
    G = D // 4
    L = w_in.shape[0]
    T = B * S
    ctab, stab = _rope_tables(positions)
    mod_shared = jax.nn.silu(c) @ mod_w + mod_b
    v_first = None
    for l in range(L):
        mod = mod_shared + mod_offset[l]
        sh1, sc1, g1, sh2, sc2, g2 = [m[:, None, :] for m in jnp.split(mod, N_MOD, axis=-1)]

        wi = w_in[l]
        o = np.cumsum((0, 2 * G, MLA_Q_RANK, MLA_KV_RANK, MLA_ROPE, 3 * G, 3 * G))
        a_parts, b_parts = [], []
        loras = [(rw_mu_x[l, 0], rw_w1[l]), (rw_mu_x[l, 1], rw_a1[l]), (rw_mu_x[l, 2], rw_g1[l])]
        if l > 0:
            loras.append((rw_vres_mu[l - 1], rw_v1[l - 1]))
        for (mu, w1), n in zip(loras, LORA_PAD):
            fa, fb = _lora_fold(mu, w1, n)
            a_parts.append(fa)
            b_parts.append(fb)
        if l == 0:
            a_parts.append(jnp.zeros((D, LORA_PAD[3]), F32))
            b_parts.append(jnp.zeros((D, LORA_PAD[3]), F32))
        w_mla = _pad_cols(wi[:, o[1]:o[4]], 768)
        w_big = jnp.concatenate([wi[:, o[0]:o[1]], w_mla, wi[:, o[4]:o[5]], wi[:, o[5]:o[6]]] + a_parts + b_parts,
                                axis=1).astype(BF16)
        widths = (2 * G, 768, 3 * G, 3 * G, 2 * sum(LORA_PAD))
        u_conv, mla_in, rkv, qkv, lora = _in_proj(x, sc1, sh1, w_big, widths)

        y_conv = _conv_module(u_conv, conv_dw[l], conv_b[l], conv_ln_g[l], conv_ln_b[l])

        wqa, wqr, wkn, wv, pa, pr = _mla_weights(mla_w_uq[l], mla_w_ukv[l])
        q_m, k_m, v_m = _mla_prep(mla_in, ctab, stab, mla_q_norm[l], mla_kv_norm[l], wqa, wqr, wkn, wv, pa, pr)
        y_mla = _mla_attn(q_m, k_m, v_m)

        w2p = _pad_rows(rw_w2[l], LORA_PAD[0]).astype(BF16)
        a2p = _pad_rows(rw_a2[l], LORA_PAD[1]).astype(BF16)
        vres = None
        if l > 0:
            vres = (v_first, rw_v0[l - 1], _pad_rows(rw_v2[l - 1], LORA_PAD[3]).astype(BF16))
        r_, k_, v_, kk_, a_, lw_, g_ = _rw_prep(rkv, lora, rw_mu_rkv[l].reshape(-1), rw_w0[l], w2p, rw_a0[l], a2p,
                                                rw_g2[l].astype(BF16), rw_k_k[l], rw_k_a[l], vres)
        if l == 0:
            v_first = v_
        y_rw = _rw_scan(r_, k_, v_, kk_, a_, lw_, g_, rw_r_k[l].reshape(-1), rw_gn_g[l], rw_gn_b[l], HIGHEST)

        pad = ((0, 0), (CA_LEFT * CHUNK, 0), (0, 0))
        qa = qkv[:, :, :G].astype(BF16)
        kp = jnp.pad(qkv[:, :, G:2 * G].astype(BF16), pad)
        vp = jnp.pad(qkv[:, :, 2 * G:].astype(BF16), pad)
        y_ca = _band_attn(qa, kp, vp, _band_bias(ca_rel_bias[l]))

        x1, h2, logits = _out_proj((y_conv, y_mla, y_rw, y_ca), w_out[l].astype(BF16), x, g1,
                                   ln_post_g[l, 0], ln_post_b[l, 0], sc2, sh2, moe_router_w[l], moe_router_b[l])

        slot_tok, slot_w, pos, tile_e, n_live = _route(logits.reshape(T, N_EXPERTS), TM_MOE)
        xs = jnp.take(h2.reshape(T, D), slot_tok, axis=0)
        ys = _moe_experts(tile_e, n_live, xs, moe_w1[l], moe_b1[l], moe_w2[l], moe_b2[l], slot_w)
        y4 = jnp.take(ys, pos, axis=0).reshape(B, S, TOP_K * D)
        x = _final(x1, y4, g2, ln_post_g[l, 1], ln_post_b[l, 1])
    return x
```

```python
import functools

import numpy as np
import jax
import jax.numpy as jnp
from jax import lax
from jax.experimental import pallas as pl
from jax.experimental.pallas import tpu as pltpu

F32 = jnp.float32
BF16 = jnp.bfloat16
HIGHEST = lax.Precision.HIGHEST
RW_PREC = HIGHEST

DEPTH = 2
CHUNK = 64
LN_EPS = 1e-5
DN_ALPHA = (2 * DEPTH) ** 0.25
N_MOD = 6
CONV_WIDTH = 31
CONV_HALO = 32
MLA_HEADS = 4
MLA_NOPE = 128
MLA_ROPE = 64
MLA_VDIM = 128
MLA_QK_PAD = 256
MLA_Q_RANK = 384
MLA_KV_RANK = 256
ROPE_THETA = 10000.0
RW_HEAD = 64
RW_GN_EPS = 64e-5
RW_GROUP_HEADS = 4
CA_HEADS = 4
CA_LEFT = 8
REL_CLIP = 256
N_EXPERTS = 32
TOP_K = 4
SWIGLU_LIMIT = 7.0
SWIGLU_ALPHA = 1.702
LORA_PAD = (128, 128, 256, 128)
V7X_VMEM_LIMIT = 56 * 1024 * 1024

TM_PROJ = 256
TM_CONV = 256
TM_PREP = 512
TQ_MLA = 256
TM_BAND = 256
TM_OUT = 256
TM_MOE = 512
TF_MOE = 256
TM_FIN = 256


def _cparams(sem, vmem=V7X_VMEM_LIMIT):
    return pltpu.CompilerParams(dimension_semantics=sem, vmem_limit_bytes=vmem)


def _ln_rows(x):
    mu = jnp.mean(x, axis=-1, keepdims=True)
    xc = x - mu
    var = jnp.mean(xc * xc, axis=-1, keepdims=True)
    return xc * lax.rsqrt(var + LN_EPS)


def _sigmoid(x):
    return 1.0 / (1.0 + jnp.exp(-x))


def _in_proj_kernel(x_ref, sc_ref, sh_ref, w_ref, *out_refs, widths):
    h = _ln_rows(x_ref[0]) * (1.0 + sc_ref[0]) + sh_ref[0]
    hb = h.astype(BF16)
    off = 0
    for o_ref, wd in zip(out_refs, widths):
        o_ref[0] = jnp.dot(hb, w_ref[:, off:off + wd], preferred_element_type=F32).astype(o_ref.dtype)
        off += wd


def _in_proj(x, sc, sh, w_big, widths):
    B, S, D = x.shape
    tm = TM_PROJ
    return pl.pallas_call(
        functools.partial(_in_proj_kernel, widths=widths),
        grid=(B, S // tm),
        in_specs=[
            pl.BlockSpec((1, tm, D), lambda b, i: (b, i, 0)),
            pl.BlockSpec((1, 1, D), lambda b, i: (b, 0, 0)),
            pl.BlockSpec((1, 1, D), lambda b, i: (b, 0, 0)),
            pl.BlockSpec(memory_space=pltpu.VMEM),
        ],
        out_specs=[pl.BlockSpec((1, tm, wd), lambda b, i: (b, i, 0)) for wd in widths],
        out_shape=[jax.ShapeDtypeStruct((B, S, wd), F32) for wd in widths],
        compiler_params=_cparams(("parallel", "parallel")),
    )(x, sc, sh, w_big)


def _conv_kernel(u_ref, halo_ref, dw_ref, db_ref, g_ref, b_ref, o_ref, scr, *, rows):
    G = o_ref.shape[-1]
    tm = o_ref.shape[1]
    i = pl.program_id(1)

    def glu(u):
        return u[:, :G] * _sigmoid(u[:, G:])

    scr[0:CONV_HALO, :] = jnp.where(i > 0, glu(halo_ref[0]), 0.0)
    scr[CONV_HALO:, :] = glu(u_ref[0])
    lead = CONV_HALO - (CONV_WIDTH - 1)

    for r0 in range(0, tm, rows):
        acc = jnp.zeros((rows, G), F32) + db_ref[...]
        for j in range(CONV_WIDTH):
            acc = acc + dw_ref[j:j + 1, :] * scr[r0 + lead + j:r0 + lead + j + rows, :]
        y = _ln_rows(acc) * g_ref[...] + b_ref[...]
        o_ref[0, r0:r0 + rows, :] = (y * _sigmoid(y)).astype(o_ref.dtype)


def _conv_module(u2, dw, db, ln_g, ln_b):
    B, S, G2 = u2.shape
    G = G2 // 2
    tm = TM_CONV
    nh = tm // CONV_HALO
    return pl.pallas_call(
        functools.partial(_conv_kernel, rows=32),
        grid=(B, S // tm),
        in_specs=[
            pl.BlockSpec((1, tm, G2), lambda b, i: (b, i, 0)),
            pl.BlockSpec((1, CONV_HALO, G2), lambda b, i: (b, jnp.maximum(i * nh - 1, 0), 0)),
            pl.BlockSpec((CONV_WIDTH, G), lambda b, i: (0, 0)),
            pl.BlockSpec((1, G), lambda b, i: (0, 0)),
            pl.BlockSpec((1, G), lambda b, i: (0, 0)),
            pl.BlockSpec((1, G), lambda b, i: (0, 0)),
        ],
        out_specs=pl.BlockSpec((1, tm, G), lambda b, i: (b, i, 0)),
        out_shape=jax.ShapeDtypeStruct((B, S, G), BF16),
        scratch_shapes=[pltpu.VMEM((CONV_HALO + tm, G), F32)],
        compiler_params=_cparams(("parallel", "parallel")),
    )(u2, u2, dw, db.reshape(1, G), ln_g.reshape(1, G), ln_b.reshape(1, G))


def _mla_prep_kernel(m_ref, ct_ref, st_ref, qn_ref, kvn_ref, wqa_ref, wqr_ref, wkn_ref, wv_ref,
                     pa_ref, pr_ref, q_ref, k_ref, v_ref, *, scale):
    m = m_ref[0]
    cq = m[:, :MLA_Q_RANK]
    ckv = m[:, MLA_Q_RANK:MLA_Q_RANK + MLA_KV_RANK]
    kpe = m[:, MLA_Q_RANK + MLA_KV_RANK:MLA_Q_RANK + MLA_KV_RANK + 128]
    ct = ct_ref[0]
    st = st_ref[0]

    def rms(t, g):
        return t * lax.rsqrt(jnp.mean(t * t, axis=-1, keepdims=True) + 1e-6) * g

    cqn = rms(cq, qn_ref[...]).astype(BF16)
    ckvn = rms(ckv, kvn_ref[...]).astype(BF16)
    qa = jnp.dot(cqn, wqa_ref[...], preferred_element_type=F32)
    qr = jnp.dot(cqn, wqr_ref[...], preferred_element_type=F32)
    kn = jnp.dot(ckvn, wkn_ref[...], preferred_element_type=F32)
    v_ref[0] = jnp.dot(ckvn, wv_ref[...], preferred_element_type=F32).astype(v_ref.dtype)
    kp = (jnp.dot(kpe, pa_ref[...], precision=HIGHEST, preferred_element_type=F32) * ct
          + jnp.dot(kpe, pr_ref[...], precision=HIGHEST, preferred_element_type=F32) * st)
    W = MLA_QK_PAD
    for h in range(MLA_HEADS):
        sl = slice(h * W, (h + 1) * W)
        q_ref[0, :, sl] = ((qa[:, sl] * ct + qr[:, sl] * st) * scale).astype(q_ref.dtype)
        k_ref[0, :, sl] = (kn[:, sl] + kp).astype(k_ref.dtype)


def _mla_prep(mla_in, ctab, stab, q_norm, kv_norm, wqa, wqr, wkn, wv, pa, pr):
    B, S, Wm = mla_in.shape
    tm = TM_PREP
    HW = MLA_HEADS * MLA_QK_PAD
    HV = MLA_HEADS * MLA_VDIM
    full = lambda a: pl.BlockSpec(a.shape, lambda b, i: (0,) * a.ndim)
    qn = q_norm.reshape(1, -1)
    kvn = kv_norm.reshape(1, -1)
    scale = (MLA_NOPE + MLA_ROPE) ** -0.5
    return pl.pallas_call(
        functools.partial(_mla_prep_kernel, scale=scale),
        grid=(B, S // tm),
        in_specs=[
            pl.BlockSpec((1, tm, Wm), lambda b, i: (b, i, 0)),
            pl.BlockSpec((1, tm, MLA_QK_PAD), lambda b, i: (b, i, 0)),
            pl.BlockSpec((1, tm, MLA_QK_PAD), lambda b, i: (b, i, 0)),
            full(qn), full(kvn), full(wqa), full(wqr), full(wkn), full(wv), full(pa), full(pr),
        ],
        out_specs=[
            pl.BlockSpec((1, tm, HW), lambda b, i: (b, i, 0)),
            pl.BlockSpec((1, tm, HW), lambda b, i: (b, i, 0)),
            pl.BlockSpec((1, tm, HV), lambda b, i: (b, i, 0)),
        ],
        out_shape=[
            jax.ShapeDtypeStruct((B, S, HW), BF16),
            jax.ShapeDtypeStruct((B, S, HW), BF16),
            jax.ShapeDtypeStruct((B, S, HV), BF16),
        ],
        compiler_params=_cparams(("parallel", "parallel")),
    )(mla_in, ctab, stab, qn, kvn, wqa, wqr, wkn, wv, pa, pr)


def _mla_attn_kernel(q_ref, k_ref, v_ref, o_ref):
    tq = q_ref.shape[1]
    i = pl.program_id(2)
    q = q_ref[0]
    row_c = lax.broadcasted_iota(jnp.int32, (tq, tq), 0) // CHUNK
    col_c = lax.broadcasted_iota(jnp.int32, (tq, tq), 1) // CHUNK
    diag_ok = col_c <= row_c

    def step(j, carry, masked):
        m, l, acc = carry
        r0 = pl.multiple_of(j * tq, tq)
        k = k_ref[0, pl.ds(r0, tq), :]
        v = v_ref[0, pl.ds(r0, tq), :]
        s = lax.dot_general(q, k, (((1,), (1,)), ((), ())), preferred_element_type=F32)
        if masked:
            s = jnp.where(diag_ok, s, -jnp.inf)
        m_new = jnp.maximum(m, jnp.max(s, axis=-1, keepdims=True))
        alpha = jnp.exp(m - m_new)
        p = jnp.exp(s - m_new)
        l = alpha * l + jnp.sum(p, axis=-1, keepdims=True)
        acc = alpha * acc + jnp.dot(p.astype(BF16), v, preferred_element_type=F32)
        return m_new, l, acc

    init = (jnp.full((tq, 1), -jnp.inf, F32), jnp.zeros((tq, 1), F32), jnp.zeros((tq, MLA_VDIM), F32))
    carry = lax.fori_loop(0, i, lambda j, c: step(j, c, False), init)
    _, l, acc = step(i, carry, True)
    o_ref[0] = (acc / l).astype(o_ref.dtype)


def _mla_attn(q, k, v):
    B, S, _ = q.shape
    tq = TQ_MLA
    return pl.pallas_call(
        _mla_attn_kernel,
        grid=(B, MLA_HEADS, S // tq),
        in_specs=[
            pl.BlockSpec((1, tq, MLA_QK_PAD), lambda b, h, i: (b, i, h)),
            pl.BlockSpec((1, S, MLA_QK_PAD), lambda b, h, i: (b, 0, h)),
            pl.BlockSpec((1, S, MLA_VDIM), lambda b, h, i: (b, 0, h)),
        ],
        out_specs=pl.BlockSpec((1, tq, MLA_VDIM), lambda b, h, i: (b, i, h)),
        out_shape=jax.ShapeDtypeStruct((B, S, MLA_HEADS * MLA_VDIM), BF16),
        compiler_params=_cparams(("parallel", "parallel", "arbitrary")),
    )(q, k, v)


def _band_attn_kernel(q_ref, k_ref, v_ref, bias_ref, o_ref, *, scale):
    tm = q_ref.shape[1]
    hd = q_ref.shape[2] // CA_HEADS
    W = (CA_LEFT + 1) * CHUNK
    i = pl.program_id(1)
    wchunk = lax.broadcasted_iota(jnp.int32, (CHUNK, W), 1) // CHUNK
    for c in range(tm // CHUNK):
        n = i * (tm // CHUNK) + c
        r0 = pl.multiple_of(n * CHUNK, CHUNK)
        valid = wchunk >= CA_LEFT - n
        for h in range(CA_HEADS):
            q = q_ref[0, c * CHUNK:(c + 1) * CHUNK, h * hd:(h + 1) * hd]
            k = k_ref[0, pl.ds(r0, W), h * hd:(h + 1) * hd]
            v = v_ref[0, pl.ds(r0, W), h * hd:(h + 1) * hd]
            s = lax.dot_general(q, k, (((1,), (1,)), ((), ())), preferred_element_type=F32) * scale
            s = jnp.where(valid, s + bias_ref[h], -jnp.inf)
            m = jnp.max(s, axis=-1, keepdims=True)
            p = jnp.exp(s - m)
            l = jnp.sum(p, axis=-1, keepdims=True)
            o = jnp.dot(p.astype(BF16), v, preferred_element_type=F32) / l
            o_ref[0, c * CHUNK:(c + 1) * CHUNK, h * hd:(h + 1) * hd] = o.astype(o_ref.dtype)


def _band_attn(q, kp, vp, bias):
    B, S, G = q.shape
    tm = TM_BAND
    Sp = kp.shape[1]
    hd = G // CA_HEADS
    return pl.pallas_call(
        functools.partial(_band_attn_kernel, scale=hd ** -0.5),
        grid=(B, S // tm),
        in_specs=[
            pl.BlockSpec((1, tm, G), lambda b, i: (b, i, 0)),
            pl.BlockSpec((1, Sp, G), lambda b, i: (b, 0, 0)),
            pl.BlockSpec((1, Sp, G), lambda b, i: (b, 0, 0)),
            pl.BlockSpec(bias.shape, lambda b, i: (0, 0, 0)),
        ],
        out_specs=pl.BlockSpec((1, tm, G), lambda b, i: (b, i, 0)),
        out_shape=jax.ShapeDtypeStruct((B, S, G), BF16),
        compiler_params=_cparams(("parallel", "arbitrary")),
    )(q, kp, vp, bias)


def _head_sum_matrix(n, dtype=F32):
    r = lax.broadcasted_iota(jnp.int32, (n, n), 0) // RW_HEAD
    c = lax.broadcasted_iota(jnp.int32, (n, n), 1) // RW_HEAD
    return (r == c).astype(dtype)


def _rw_prep_kernel(*refs, has_vres):
    if has_vres:
        (rkv_ref, rkvh_ref, lo_ref, loh_ref, mu_ref, w0_ref, w2_ref, a0_ref, a2_ref, g2_ref, kk_ref, ka_ref,
         vf_ref, v0_ref, v2_ref, r_o, k_o, v_o, kk_o, a_o, lw_o, g_o) = refs
    else:
        (rkv_ref, rkvh_ref, lo_ref, loh_ref, mu_ref, w0_ref, w2_ref, a0_ref, a2_ref, g2_ref, kk_ref, ka_ref,
         r_o, k_o, v_o, kk_o, a_o, lw_o, g_o) = refs
    G = r_o.shape[-1]
    tm = r_o.shape[1]
    i = pl.program_id(1)
    first = lax.broadcasted_iota(jnp.int32, (tm, 1), 0) == 0

    def shifted(cur, halo):
        prev = jnp.where(i > 0, halo[7:8, :], 0.0)
        return jnp.where(first, prev, pltpu.roll(cur, 1, axis=0))

    rkv = rkv_ref[0]
    rkv = rkv + (shifted(rkv, rkvh_ref[0]) - rkv) * mu_ref[...]
    r = rkv[:, :G]
    k = rkv[:, G:2 * G]
    v = rkv[:, 2 * G:]

    lo = lo_ref[0]
    LP = sum(LORA_PAD)
    lin = lo[:, :LP] + shifted(lo[:, LP:], loh_ref[0][:, LP:])
    o1 = LORA_PAD[0]
    o2 = o1 + LORA_PAD[1]
    o3 = o2 + LORA_PAD[2]
    w_raw = w0_ref[...] + jnp.dot(jnp.tanh(lin[:, :o1]).astype(BF16), w2_ref[...], preferred_element_type=F32)
    w = -(jnp.maximum(-w_raw, 0.0) + jnp.log(1.0 + jnp.exp(-jnp.abs(w_raw)))) - 0.5
    lw_o[0] = -jnp.exp(w)
    a = _sigmoid(a0_ref[...] + jnp.dot(lin[:, o1:o2].astype(BF16), a2_ref[...], preferred_element_type=F32))
    g_o[0] = jnp.dot(_sigmoid(lin[:, o2:o3]).astype(BF16), g2_ref[...], preferred_element_type=F32)
    kk = k * kk_ref[...]
    n2 = jnp.dot(kk * kk, _head_sum_matrix(G), precision=HIGHEST, preferred_element_type=F32)
    kk_o[0] = kk / jnp.maximum(jnp.sqrt(n2), 1e-12)
    k_o[0] = k * (1.0 + (a - 1.0) * ka_ref[...])
    if has_vres:
        mix = _sigmoid(v0_ref[...] + jnp.dot(lin[:, o3:].astype(BF16), v2_ref[...], preferred_element_type=F32))
        v = v + (vf_ref[0] - v) * mix
    r_o[0] = r
    v_o[0] = v
    a_o[0] = a


def _rw_prep(rkv, lora, mu_rkv, w0, w2p, a0, a2p, g2, k_k, k_a, vres):
    B, S, G3 = rkv.shape
    G = G3 // 3
    LW = lora.shape[-1]
    tm = TM_PREP
    nh = tm // 8
    row = lambda a: a.reshape(1, -1)
    full = lambda a: pl.BlockSpec(a.shape, lambda b, i: (0,) * a.ndim)
    halo = lambda w: pl.BlockSpec((1, 8, w), lambda b, i: (b, jnp.maximum(i * nh - 1, 0), 0))
    tile = lambda w: pl.BlockSpec((1, tm, w), lambda b, i: (b, i, 0))
    args = [rkv, rkv, lora, lora, row(mu_rkv), row(w0), w2p, row(a0), a2p, g2, row(k_k), row(k_a)]
    specs = [tile(G3), halo(G3), tile(LW), halo(LW)] + [full(a) for a in args[4:]]
    if vres is not None:
        v_first, v0, v2p = vres
        extra = [v_first, row(v0), v2p]
        args += extra
        specs += [tile(G), full(extra[1]), full(extra[2])]
    return pl.pallas_call(
        functools.partial(_rw_prep_kernel, has_vres=vres is not None),
        grid=(B, S // tm),
        in_specs=specs,
        out_specs=[tile(G)] * 7,
        out_shape=[jax.ShapeDtypeStruct((B, S, G), F32)] * 7,
        compiler_params=_cparams(("parallel", "parallel")),
    )(*args)


def _rw_group(r, k, v, kk, a, lw, st0, prec):
    C, N = r.shape
    H = N // RW_HEAD
    dg = lambda x, y, dims: lax.dot_general(x.astype(BF16), y.astype(BF16), (dims, ((), ())), preferred_element_type=F32)
    mm = lambda x, y: dg(x, y, ((1,), (0,)))
    nt = lambda x, y: dg(x, y, ((1,), (1,)))
    tn = lambda x, y: dg(x.T, y, ((1,), (0,)))
    mm_state = lambda x, y: jnp.dot(x, y, precision=prec, preferred_element_type=F32)

    tr = lax.broadcasted_iota(jnp.int32, (C, C), 0)
    tc = lax.broadcasted_iota(jnp.int32, (C, C), 1)
    cum = jnp.dot((tc <= tr).astype(F32), lw, precision=HIGHEST, preferred_element_type=F32)
    tot = cum[C - 1:C, :]
    e_pos = jnp.exp(cum)
    e_neg = jnp.exp(-cum)
    e_end = jnp.exp(tot - cum)
    b = kk * a
    Aa = jnp.exp(cum - lw) * kk
    Bb = e_neg * b
    Kk = e_neg * k
    Rr = e_pos * r
    Bh = e_end * b
    Kh = e_end * k

    sr = lax.broadcasted_iota(jnp.int32, (H * C, N), 0)
    sc = lax.broadcasted_iota(jnp.int32, (H * C, N), 1)
    hmask = (sr // C) == (sc // RW_HEAD)
    tile = lambda x: jnp.concatenate([x] * H, axis=0)
    stack = lambda x: jnp.where(hmask, tile(x), 0.0)
    collapse = lambda xs: sum(xs[h * C:(h + 1) * C, :] for h in range(H))
    As, Rs, Vs = stack(Aa), stack(Rr), stack(v)
    Bt, Kt = tile(Bb), tile(Kk)

    br = lax.broadcasted_iota(jnp.int32, (H * C, H * C), 0)
    bc = lax.broadcasted_iota(jnp.int32, (H * C, H * C), 1)
    same = (br // C) == (bc // C)
    strict = same & ((br % C) > (bc % C))
    incl = same & ((br % C) >= (bc % C))
    eye = br == bc
    Lab = jnp.where(strict, nt(As, Bt), 0.0)
    Lak = jnp.where(strict, nt(As, Kt), 0.0)
    Mrb = jnp.where(incl, nt(Rs, Bt), 0.0)
    Mrk = jnp.where(incl, nt(Rs, Kt), 0.0)

    M = -Lab
    X = jnp.where(eye, 1.0, M)
    P = mm(M, M)
    steps = int(np.log2(C)) - 1
    for s in range(steps):
        X = X + mm(X, P)
        if s + 1 < steps:
            P = mm(P, P)
    W1s = mm(X, As)
    W2s = mm(X, mm(Lak, Vs))
    Q = collapse(Rs - mm(Mrb, W1s))
    Z = collapse(mm(Mrk, Vs) - mm(Mrb, W2s))
    W1 = collapse(W1s)
    W2 = collapse(W2s)
    kr = lax.broadcasted_iota(jnp.int32, (N, N), 0)
    kc = lax.broadcasted_iota(jnp.int32, (N, N), 1)
    bd = (kr // RW_HEAD) == (kc // RW_HEAD)
    Gm = jnp.where(kr == kc, jnp.exp(tot), 0.0) - jnp.where(bd, tn(Bh, W1), 0.0)
    Hm = jnp.where(bd, tn(Kh, v) - tn(Bh, W2), 0.0)
    y = mm_state(Q, st0) + Z
    st1 = mm_state(Gm, st0) + Hm
    return y, st1


def _rw_scan_kernel(r_ref, k_ref, v_ref, kk_ref, a_ref, lw_ref, g_ref, rk_ref, gg_ref, gb_ref, o_ref, st_ref, *, prec):
    n = pl.program_id(1)
    N = RW_GROUP_HEADS * RW_HEAD

    @pl.when(n == 0)
    def _():
        st_ref[...] = jnp.zeros_like(st_ref)

    hs = _head_sum_matrix(N)
    for gi in range(r_ref.shape[-1] // N):
        sl = slice(gi * N, (gi + 1) * N)
        r, k, v = r_ref[0, :, sl], k_ref[0, :, sl], v_ref[0, :, sl]
        y, st1 = _rw_group(r, k, v, kk_ref[0, :, sl], a_ref[0, :, sl], lw_ref[0, :, sl], st_ref[gi], prec)
        st_ref[gi] = st1
        hmean = lambda t: jnp.dot(t, hs, precision=HIGHEST, preferred_element_type=F32) * (1.0 / RW_HEAD)
        yc = y - hmean(y)
        yn = yc * lax.rsqrt(hmean(yc * yc) + RW_GN_EPS) * gg_ref[:, sl] + gb_ref[:, sl]
        bonus = jnp.dot(r * k * rk_ref[:, sl], hs, precision=HIGHEST, preferred_element_type=F32) * v
        o_ref[0, :, sl] = ((yn + bonus) * g_ref[0, :, sl]).astype(o_ref.dtype)


def _rw_scan(r, k, v, kk, a, lw, g, r_k, gn_g, gn_b, prec):
    B, S, G = r.shape
    N = RW_GROUP_HEADS * RW_HEAD
    tile = pl.BlockSpec((1, CHUNK, G), lambda b, n: (b, n, 0))
    rowspec = pl.BlockSpec((1, G), lambda b, n: (0, 0))
    return pl.pallas_call(
        functools.partial(_rw_scan_kernel, prec=prec),
        grid=(B, S // CHUNK),
        in_specs=[tile] * 7 + [rowspec] * 3,
        out_specs=tile,
        out_shape=jax.ShapeDtypeStruct((B, S, G), BF16),
        scratch_shapes=[pltpu.VMEM((G // N, N, N), F32)],
        compiler_params=_cparams(("parallel", "arbitrary")),
    )(r, k, v, kk, a, lw, g, r_k.reshape(1, G), gn_g.reshape(1, G), gn_b.reshape(1, G))


def _out_proj_kernel(yc_ref, ym_ref, yr_ref, ya_ref, w_ref, x_ref, g1_ref, lg_ref, lb_ref, sc_ref, sh_ref,
                     rw_ref, rb_ref, x1_ref, h2_ref, lo_ref):
    G = yc_ref.shape[-1]
    y = jnp.zeros(x_ref.shape[1:], F32)
    for j, y_ref in enumerate((yc_ref, ym_ref, yr_ref, ya_ref)):
        y = y + jnp.dot(y_ref[0], w_ref[j * G:(j + 1) * G, :], preferred_element_type=F32)
    x1 = _ln_rows(DN_ALPHA * x_ref[0] + (1.0 + g1_ref[0]) * y) * lg_ref[...] + lb_ref[...]
    x1_ref[0] = x1
    h2 = _ln_rows(x1) * (1.0 + sc_ref[0]) + sh_ref[0]
    h2_ref[0] = h2.astype(h2_ref.dtype)
    lo_ref[0] = jnp.dot(h2, rw_ref[...], precision=HIGHEST, preferred_element_type=F32) + rb_ref[...]


def _out_proj(ys, w_out, x, g1, ln_g, ln_b, sc2, sh2, router_w, router_b):
    B, S, D = x.shape
    G = ys[0].shape[-1]
    E = router_w.shape[-1]
    tm = TM_OUT
    tile = lambda w: pl.BlockSpec((1, tm, w), lambda b, i: (b, i, 0))
    mod = pl.BlockSpec((1, 1, D), lambda b, i: (b, 0, 0))
    full = lambda a: pl.BlockSpec(a.shape, lambda b, i: (0,) * a.ndim)
    lg, lb, rb = ln_g.reshape(1, D), ln_b.reshape(1, D), router_b.reshape(1, E)
    return pl.pallas_call(
        _out_proj_kernel,
        grid=(B, S // tm),
        in_specs=[tile(G)] * 4 + [pl.BlockSpec(memory_space=pltpu.VMEM), tile(D), mod, full(lg), full(lb), mod, mod,
                                  full(router_w), full(rb)],
        out_specs=[tile(D), tile(D), tile(E)],
        out_shape=[jax.ShapeDtypeStruct((B, S, D), F32), jax.ShapeDtypeStruct((B, S, D), BF16),
                   jax.ShapeDtypeStruct((B, S, E), F32)],
        compiler_params=_cparams(("parallel", "parallel")),
    )(*ys, w_out, x, g1, lg, lb, sc2, sh2, router_w, rb)


def _moe_kernel(te_ref, nt_ref, x_ref, wg_ref, wl_ref, bg_ref, bl_ref, w2_ref, b2_ref, sw_ref, o_ref, acc_ref):
    i = pl.program_id(0)
    f = pl.program_id(1)
    nf = pl.num_programs(1)
    live = i < nt_ref[0]

    @pl.when(f == 0)
    def _():
        acc_ref[...] = jnp.zeros_like(acc_ref)

    @pl.when(live)
    def _():
        x = x_ref[...]
        glu = jnp.dot(x, wg_ref[0].astype(BF16), preferred_element_type=F32) + bg_ref[0]
        lin = jnp.dot(x, wl_ref[0].astype(BF16), preferred_element_type=F32) + bl_ref[0]
        glu = jnp.minimum(glu, SWIGLU_LIMIT)
        lin = jnp.clip(lin, -SWIGLU_LIMIT, SWIGLU_LIMIT)
        act = glu * _sigmoid(SWIGLU_ALPHA * glu) * (lin + 1.0)
        acc_ref[...] += jnp.dot(act.astype(BF16), w2_ref[0].astype(BF16), preferred_element_type=F32)

    @pl.when(f == nf - 1)
    def _():
        o_ref[...] = ((acc_ref[...] + b2_ref[0]) * sw_ref[...]).astype(o_ref.dtype)


def _moe_experts(tile_e, n_live, xs, w1, b1, w2, b2, slot_w, layer):
    n_slots, D = xs.shape
    L, E, _, F2 = w1.shape
    F = F2 // 2
    tm, tf = TM_MOE, TF_MOE
    nf = F // tf
    w1 = w1.reshape(L * E, D, F2)
    w2 = w2.reshape(L * E, F, D)
    b1r = b1.reshape(L * E, 1, F2)
    b2r = b2.reshape(L * E, 1, D)
    tile_e = tile_e + layer * E
    fidx = lambda i, f, nt: jnp.where(i < nt[0], f, nf - 1)
    grid_spec = pltpu.PrefetchScalarGridSpec(
        num_scalar_prefetch=2,
        grid=(n_slots // tm, nf),
        in_specs=[
            pl.BlockSpec((tm, D), lambda i, f, te, nt: (i, 0)),
            pl.BlockSpec((1, D, tf), lambda i, f, te, nt: (te[i], 0, fidx(i, f, nt))),
            pl.BlockSpec((1, D, tf), lambda i, f, te, nt: (te[i], 0, nf + fidx(i, f, nt))),
            pl.BlockSpec((1, 1, tf), lambda i, f, te, nt: (te[i], 0, fidx(i, f, nt))),
            pl.BlockSpec((1, 1, tf), lambda i, f, te, nt: (te[i], 0, nf + fidx(i, f, nt))),
            pl.BlockSpec((1, tf, D), lambda i, f, te, nt: (te[i], fidx(i, f, nt), 0)),
            pl.BlockSpec((1, 1, D), lambda i, f, te, nt: (te[i], 0, 0)),
            pl.BlockSpec((tm, 1), lambda i, f, te, nt: (i, 0)),
        ],
        out_specs=pl.BlockSpec((tm, D), lambda i, f, te, nt: (i, 0)),
        scratch_shapes=[pltpu.VMEM((tm, D), F32)],
    )
    return pl.pallas_call(
        _moe_kernel,
        grid_spec=grid_spec,
        out_shape=jax.ShapeDtypeStruct((n_slots, D), F32),
        compiler_params=_cparams(("arbitrary", "arbitrary")),
    )(tile_e, n_live, xs, w1, w1, b1r, b1r, w2, b2r, slot_w.reshape(n_slots, 1))


def _final_kernel(x_ref, y0_ref, y1_ref, y2_ref, y3_ref, g2_ref, lg_ref, lb_ref, o_ref):
    y = (y0_ref[0] + y1_ref[0]) + (y2_ref[0] + y3_ref[0])
    o_ref[0] = _ln_rows(DN_ALPHA * x_ref[0] + (1.0 + g2_ref[0]) * y) * lg_ref[...] + lb_ref[...]


def _final(x1, ys, g2, ln_g, ln_b):
    B, S, D = x1.shape
    tm = TM_FIN
    tile = pl.BlockSpec((1, tm, D), lambda b, i: (b, i, 0))
    return pl.pallas_call(
        _final_kernel,
        grid=(B, S // tm),
        in_specs=[tile] * (1 + TOP_K) + [
            pl.BlockSpec((1, 1, D), lambda b, i: (b, 0, 0)),
            pl.BlockSpec((1, D), lambda b, i: (0, 0)),
            pl.BlockSpec((1, D), lambda b, i: (0, 0)),
        ],
        out_specs=tile,
        out_shape=jax.ShapeDtypeStruct((B, S, D), F32),
        compiler_params=_cparams(("parallel", "parallel")),
    )(x1, *ys, g2, ln_g.reshape(1, D), ln_b.reshape(1, D))


def _pad_cols(w, n):
    return jnp.pad(w, ((0, 0), (0, n - w.shape[1])))


def _pad_rows(w, n):
    return jnp.pad(w, ((0, n - w.shape[0]), (0, 0)))


def _lora_fold(mu, w1, n):
    return _pad_cols((1.0 - mu)[:, None] * w1, n), _pad_cols(mu[:, None] * w1, n)


def _mla_weights(w_uq, w_ukv):
    Rq = w_uq.shape[0]
    W = MLA_QK_PAD
    half = MLA_ROPE // 2
    uq = w_uq.reshape(Rq, MLA_HEADS, MLA_NOPE + MLA_ROPE)
    pe = uq[:, :, MLA_NOPE:]
    pe_rot = jnp.concatenate([-pe[:, :, half:], pe[:, :, :half]], axis=-1)
    zpad = jnp.zeros((Rq, MLA_HEADS, W - MLA_NOPE - MLA_ROPE), F32)
    wqa = jnp.concatenate([uq, zpad], axis=-1).reshape(Rq, MLA_HEADS * W)
    wqr = jnp.concatenate([jnp.zeros_like(uq[:, :, :MLA_NOPE]), pe_rot, zpad], axis=-1).reshape(Rq, MLA_HEADS * W)
    Rk = w_ukv.shape[0]
    ukv = w_ukv.reshape(Rk, MLA_HEADS, MLA_NOPE + MLA_VDIM)
    wkn = jnp.concatenate([ukv[:, :, :MLA_NOPE], jnp.zeros((Rk, MLA_HEADS, W - MLA_NOPE), F32)], axis=-1)
    wkn = wkn.reshape(Rk, MLA_HEADS * W)
    wv = ukv[:, :, MLA_NOPE:].reshape(Rk, MLA_HEADS * MLA_VDIM)
    pa = np.zeros((128, W), np.float32)
    pr = np.zeros((128, W), np.float32)
    for d in range(MLA_ROPE):
        pa[d, MLA_NOPE + d] = 1.0
    for d in range(half):
        pr[half + d, MLA_NOPE + d] = -1.0
        pr[d, MLA_NOPE + half + d] = 1.0
    return wqa.astype(BF16), wqr.astype(BF16), wkn.astype(BF16), wv.astype(BF16), jnp.asarray(pa), jnp.asarray(pr)


def _rope_tables(positions):
    B, S = positions.shape
    inv = ROPE_THETA ** (-jnp.arange(0, MLA_ROPE, 2, dtype=F32) / MLA_ROPE)
    ang = positions.astype(F32)[..., None] * inv
    cos, sin = jnp.cos(ang), jnp.sin(ang)
    one = jnp.ones((B, S, MLA_NOPE), F32)
    zero = jnp.zeros((B, S, MLA_NOPE), F32)
    zpad = jnp.zeros((B, S, MLA_QK_PAD - MLA_NOPE - MLA_ROPE), F32)
    return (jnp.concatenate([one, cos, cos, zpad], -1), jnp.concatenate([zero, sin, sin, zpad], -1))


def _band_bias(rel_table):
    W = (CA_LEFT + 1) * CHUNK
    q_pos = CA_LEFT * CHUNK + jnp.arange(CHUNK)[:, None]
    k_pos = jnp.arange(W)[None, :]
    rel = jnp.clip(q_pos - k_pos, -REL_CLIP, REL_CLIP) + REL_CLIP
    return rel_table[:, rel].astype(F32)


def _route(logits, tm):
    T = logits.shape[0]
    top_val, top_idx = lax.top_k(logits, TOP_K)
    gate = jax.nn.softmax(top_val, axis=-1)
    flat_e = top_idx.reshape(-1)
    flat_tok = jnp.repeat(jnp.arange(T, dtype=jnp.int32), TOP_K)
    order = jnp.argsort(flat_e)
    e_sorted = flat_e[order]
    counts = jnp.bincount(flat_e, length=N_EXPERTS)
    padded = (counts + tm - 1) // tm * tm
    start = jnp.cumsum(counts) - counts
    pad_end = jnp.cumsum(padded)
    pad_start = pad_end - padded
    dest = (pad_start[e_sorted] + jnp.arange(T * TOP_K) - start[e_sorted]).astype(jnp.int32)
    n_tiles = -(-(T * TOP_K + N_EXPERTS * (tm - 1)) // tm)
    n_slots = n_tiles * tm
    slot_tok = jnp.zeros((n_slots,), jnp.int32).at[dest].set(flat_tok[order])
    slot_w = jnp.zeros((n_slots,), F32).at[dest].set(gate.reshape(-1)[order])
    pos = jnp.zeros((T * TOP_K,), jnp.int32).at[order].set(dest)
    n_live = (pad_end[-1] // tm).astype(jnp.int32)
    tile_e = jnp.minimum(jnp.searchsorted(pad_end, jnp.arange(n_tiles) * tm, side='right'), N_EXPERTS - 1)
    last_e = tile_e[jnp.maximum(n_live - 1, 0)]
    tile_e = jnp.where(jnp.arange(n_tiles) < n_live, tile_e, last_e).astype(jnp.int32)
    return slot_tok, slot_w, pos, tile_e, n_live.reshape(1)


def kernel(x, c, positions, mod_w, mod_b, mod_offset, ln_post_g, ln_post_b, w_in, w_out, conv_dw, conv_b, conv_ln_g, conv_ln_b, mla_q_norm, mla_kv_norm, mla_w_uq, mla_w_ukv, rw_mu_rkv, rw_mu_x, rw_w0, rw_w1, rw_w2, rw_a0, rw_a1, rw_a2, rw_g1, rw_g2, rw_k_k, rw_k_a, rw_r_k, rw_gn_g, rw_gn_b, rw_vres_mu, rw_v0, rw_v1, rw_v2, ca_rel_bias, moe_router_w, moe_router_b, moe_w1, moe_b1, moe_w2, moe_b2):
    B, S, D = x.shape
    G = D // 4
    L = w_in.shape[0]
    T = B * S
    ctab, stab = _rope_tables(positions)
    mod_shared = jax.nn.silu(c) @ mod_w + mod_b
    v_first = None
    for l in range(L):
        mod = mod_shared + mod_offset[l]
        sh1, sc1, g1, sh2, sc2, g2 = [m[:, None, :] for m in jnp.split(mod, N_MOD, axis=-1)]

        wi = w_in[l]
        o = np.cumsum((0, 2 * G, MLA_Q_RANK, MLA_KV_RANK, MLA_ROPE, 3 * G, 3 * G))
        a_parts, b_parts = [], []
        loras = [(rw_mu_x[l, 0], rw_w1[l]), (rw_mu_x[l, 1], rw_a1[l]), (rw_mu_x[l, 2], rw_g1[l])]
        if l > 0:
            loras.append((rw_vres_mu[l - 1], rw_v1[l - 1]))
        for (mu, w1), n in zip(loras, LORA_PAD):
            fa, fb = _lora_fold(mu, w1, n)
            a_parts.append(fa)
            b_parts.append(fb)
        if l == 0:
            a_parts.append(jnp.zeros((D, LORA_PAD[3]), F32))
            b_parts.append(jnp.zeros((D, LORA_PAD[3]), F32))
        w_mla = _pad_cols(wi[:, o[1]:o[4]], 768)
        w_big = jnp.concatenate([wi[:, o[0]:o[1]], w_mla, wi[:, o[4]:o[5]], wi[:, o[5]:o[6]]] + a_parts + b_parts,
                                axis=1).astype(BF16)
        widths = (2 * G, 768, 3 * G, 3 * G, 2 * sum(LORA_PAD))
        u_conv, mla_in, rkv, qkv, lora = _in_proj(x, sc1, sh1, w_big, widths)

        y_conv = _conv_module(u_conv, conv_dw[l], conv_b[l], conv_ln_g[l], conv_ln_b[l])

        wqa, wqr, wkn, wv, pa, pr = _mla_weights(mla_w_uq[l], mla_w_ukv[l])
        q_m, k_m, v_m = _mla_prep(mla_in, ctab, stab, mla_q_norm[l], mla_kv_norm[l], wqa, wqr, wkn, wv, pa, pr)
        y_mla = _mla_attn(q_m, k_m, v_m)

        w2p = _pad_rows(rw_w2[l], LORA_PAD[0]).astype(BF16)
        a2p = _pad_rows(rw_a2[l], LORA_PAD[1]).astype(BF16)
        vres = None
        if l > 0:
            vres = (v_first, rw_v0[l - 1], _pad_rows(rw_v2[l - 1], LORA_PAD[3]).astype(BF16))
        r_, k_, v_, kk_, a_, lw_, g_ = _rw_prep(rkv, lora, rw_mu_rkv[l].reshape(-1), rw_w0[l], w2p, rw_a0[l], a2p,
                                                rw_g2[l].astype(BF16), rw_k_k[l], rw_k_a[l], vres)
        if l == 0:
            v_first = v_
        y_rw = _rw_scan(r_, k_, v_, kk_, a_, lw_, g_, rw_r_k[l].reshape(-1), rw_gn_g[l], rw_gn_b[l], RW_PREC)

        pad = ((0, 0), (CA_LEFT * CHUNK, 0), (0, 0))
        qa = qkv[:, :, :G].astype(BF16)
        kp = jnp.pad(qkv[:, :, G:2 * G].astype(BF16), pad)
        vp = jnp.pad(qkv[:, :, 2 * G:].astype(BF16), pad)
        y_ca = _band_attn(qa, kp, vp, _band_bias(ca_rel_bias[l]))

        x1, h2, logits = _out_proj((y_conv, y_mla, y_rw, y_ca), w_out[l].astype(BF16), x, g1,
                                   ln_post_g[l, 0], ln_post_b[l, 0], sc2, sh2, moe_router_w[l], moe_router_b[l])

        slot_tok, slot_w, pos, tile_e, n_live = _route(logits.reshape(T, N_EXPERTS), TM_MOE)
        xs = jnp.take(h2.reshape(T, D), slot_tok, axis=0)
        ys = _moe_experts(tile_e, n_live, xs, moe_w1, moe_b1, moe_w2, moe_b2, slot_w, l)
        pos_k = pos.reshape(T, TOP_K)
        yk = [jnp.take(ys, pos_k[:, j], axis=0).reshape(B, S, D) for j in range(TOP_K)]
        x = _final(x1, yk, g2, ln_post_g[l, 1], ln_post_b[l, 1])
    return x
```

```python
import functools

import numpy as np
import jax
import jax.numpy as jnp
from jax import lax
from jax.experimental import pallas as pl
from jax.experimental.pallas import tpu as pltpu

F32 = jnp.float32
BF16 = jnp.bfloat16
HIGHEST = lax.Precision.HIGHEST
RW_PREC = HIGHEST

DEPTH = 2
CHUNK = 64
LN_EPS = 1e-5
DN_ALPHA = (2 * DEPTH) ** 0.25
N_MOD = 6
CONV_WIDTH = 31
CONV_HALO = 32
MLA_HEADS = 4
MLA_NOPE = 128
MLA_ROPE = 64
MLA_VDIM = 128
MLA_QK_PAD = 256
MLA_Q_RANK = 384
MLA_KV_RANK = 256
ROPE_THETA = 10000.0
RW_HEAD = 64
RW_GN_EPS = 64e-5
RW_GROUP_HEADS = 4
CA_HEADS = 4
CA_LEFT = 8
REL_CLIP = 256
N_EXPERTS = 32
TOP_K = 4
SWIGLU_LIMIT = 7.0
SWIGLU_ALPHA = 1.702
LORA_PAD = (128, 128, 256, 128)
V7X_VMEM_LIMIT = 56 * 1024 * 1024

TM_PROJ = 256
TM_CONV = 256
TM_PREP = 512
TQ_MLA = 256
TM_BAND = 256
TM_OUT = 256
TM_MOE = 1024
TF_MOE = 256
TM_FIN = 256


def _cparams(sem, vmem=V7X_VMEM_LIMIT):
    return pltpu.CompilerParams(dimension_semantics=sem, vmem_limit_bytes=vmem)


def _ln_rows(x):
    mu = jnp.mean(x, axis=-1, keepdims=True)
    xc = x - mu
    var = jnp.mean(xc * xc, axis=-1, keepdims=True)
    return xc * lax.rsqrt(var + LN_EPS)


def _sigmoid(x):
    return 1.0 / (1.0 + jnp.exp(-x))


def _in_proj_kernel(x_ref, sc_ref, sh_ref, w_ref, *out_refs, widths):
    h = _ln_rows(x_ref[0]) * (1.0 + sc_ref[0]) + sh_ref[0]
    hb = h.astype(BF16)
    off = 0
    for o_ref, wd in zip(out_refs, widths):
        o_ref[0] = jnp.dot(hb, w_ref[:, off:off + wd], preferred_element_type=F32).astype(o_ref.dtype)
        off += wd


def _in_proj(x, sc, sh, w_big, widths):
    B, S, D = x.shape
    tm = TM_PROJ
    return pl.pallas_call(
        functools.partial(_in_proj_kernel, widths=widths),
        grid=(B, S // tm),
        in_specs=[
            pl.BlockSpec((1, tm, D), lambda b, i: (b, i, 0)),
            pl.BlockSpec((1, 1, D), lambda b, i: (b, 0, 0)),
            pl.BlockSpec((1, 1, D), lambda b, i: (b, 0, 0)),
            pl.BlockSpec(memory_space=pltpu.VMEM),
        ],
        out_specs=[pl.BlockSpec((1, tm, wd), lambda b, i: (b, i, 0)) for wd in widths],
        out_shape=[jax.ShapeDtypeStruct((B, S, wd), F32) for wd in widths],
        compiler_params=_cparams(("parallel", "parallel")),
    )(x, sc, sh, w_big)


def _conv_kernel(u_ref, halo_ref, dw_ref, db_ref, g_ref, b_ref, o_ref, scr, *, rows):
    G = o_ref.shape[-1]
    tm = o_ref.shape[1]
    i = pl.program_id(1)

    def glu(u):
        return u[:, :G] * _sigmoid(u[:, G:])

    scr[0:CONV_HALO, :] = jnp.where(i > 0, glu(halo_ref[0]), 0.0)
    scr[CONV_HALO:, :] = glu(u_ref[0])
    lead = CONV_HALO - (CONV_WIDTH - 1)

    for r0 in range(0, tm, rows):
        acc = jnp.zeros((rows, G), F32) + db_ref[...]
        for j in range(CONV_WIDTH):
            acc = acc + dw_ref[j:j + 1, :] * scr[r0 + lead + j:r0 + lead + j + rows, :]
        y = _ln_rows(acc) * g_ref[...] + b_ref[...]
        o_ref[0, r0:r0 + rows, :] = (y * _sigmoid(y)).astype(o_ref.dtype)


def _conv_module(u2, dw, db, ln_g, ln_b):
    B, S, G2 = u2.shape
    G = G2 // 2
    tm = TM_CONV
    nh = tm // CONV_HALO
    return pl.pallas_call(
        functools.partial(_conv_kernel, rows=32),
        grid=(B, S // tm),
        in_specs=[
            pl.BlockSpec((1, tm, G2), lambda b, i: (b, i, 0)),
            pl.BlockSpec((1, CONV_HALO, G2), lambda b, i: (b, jnp.maximum(i * nh - 1, 0), 0)),
            pl.BlockSpec((CONV_WIDTH, G), lambda b, i: (0, 0)),
            pl.BlockSpec((1, G), lambda b, i: (0, 0)),
            pl.BlockSpec((1, G), lambda b, i: (0, 0)),
            pl.BlockSpec((1, G), lambda b, i: (0, 0)),
        ],
        out_specs=pl.BlockSpec((1, tm, G), lambda b, i: (b, i, 0)),
        out_shape=jax.ShapeDtypeStruct((B, S, G), BF16),
        scratch_shapes=[pltpu.VMEM((CONV_HALO + tm, G), F32)],
        compiler_params=_cparams(("parallel", "parallel")),
    )(u2, u2, dw, db.reshape(1, G), ln_g.reshape(1, G), ln_b.reshape(1, G))


def _mla_prep_kernel(m_ref, ct_ref, st_ref, qn_ref, kvn_ref, wqa_ref, wqr_ref, wkn_ref, wv_ref,
                     pa_ref, pr_ref, q_ref, k_ref, v_ref, *, scale):
    m = m_ref[0]
    cq = m[:, :MLA_Q_RANK]
    ckv = m[:, MLA_Q_RANK:MLA_Q_RANK + MLA_KV_RANK]
    kpe = m[:, MLA_Q_RANK + MLA_KV_RANK:MLA_Q_RANK + MLA_KV_RANK + 128]
    ct = ct_ref[0]
    st = st_ref[0]

    def rms(t, g):
        return t * lax.rsqrt(jnp.mean(t * t, axis=-1, keepdims=True) + 1e-6) * g

    cqn = rms(cq, qn_ref[...]).astype(BF16)
    ckvn = rms(ckv, kvn_ref[...]).astype(BF16)
    qa = jnp.dot(cqn, wqa_ref[...], preferred_element_type=F32)
    qr = jnp.dot(cqn, wqr_ref[...], preferred_element_type=F32)
    kn = jnp.dot(ckvn, wkn_ref[...], preferred_element_type=F32)
    v_ref[0] = jnp.dot(ckvn, wv_ref[...], preferred_element_type=F32).astype(v_ref.dtype)
    kp = (jnp.dot(kpe, pa_ref[...], precision=HIGHEST, preferred_element_type=F32) * ct
          + jnp.dot(kpe, pr_ref[...], precision=HIGHEST, preferred_element_type=F32) * st)
    W = MLA_QK_PAD
    for h in range(MLA_HEADS):
        sl = slice(h * W, (h + 1) * W)
        q_ref[0, :, sl] = ((qa[:, sl] * ct + qr[:, sl] * st) * scale).astype(q_ref.dtype)
        k_ref[0, :, sl] = (kn[:, sl] + kp).astype(k_ref.dtype)


def _mla_prep(mla_in, ctab, stab, q_norm, kv_norm, wqa, wqr, wkn, wv, pa, pr):
    B, S, Wm = mla_in.shape
    tm = TM_PREP
    HW = MLA_HEADS * MLA_QK_PAD
    HV = MLA_HEADS * MLA_VDIM
    full = lambda a: pl.BlockSpec(a.shape, lambda b, i: (0,) * a.ndim)
    qn = q_norm.reshape(1, -1)
    kvn = kv_norm.reshape(1, -1)
    scale = (MLA_NOPE + MLA_ROPE) ** -0.5
    return pl.pallas_call(
        functools.partial(_mla_prep_kernel, scale=scale),
        grid=(B, S // tm),
        in_specs=[
            pl.BlockSpec((1, tm, Wm), lambda b, i: (b, i, 0)),
            pl.BlockSpec((1, tm, MLA_QK_PAD), lambda b, i: (b, i, 0)),
            pl.BlockSpec((1, tm, MLA_QK_PAD), lambda b, i: (b, i, 0)),
            full(qn), full(kvn), full(wqa), full(wqr), full(wkn), full(wv), full(pa), full(pr),
        ],
        out_specs=[
            pl.BlockSpec((1, tm, HW), lambda b, i: (b, i, 0)),
            pl.BlockSpec((1, tm, HW), lambda b, i: (b, i, 0)),
            pl.BlockSpec((1, tm, HV), lambda b, i: (b, i, 0)),
        ],
        out_shape=[
            jax.ShapeDtypeStruct((B, S, HW), BF16),
            jax.ShapeDtypeStruct((B, S, HW), BF16),
            jax.ShapeDtypeStruct((B, S, HV), BF16),
        ],
        compiler_params=_cparams(("parallel", "parallel")),
    )(mla_in, ctab, stab, qn, kvn, wqa, wqr, wkn, wv, pa, pr)


def _mla_attn_kernel(q_ref, k_ref, v_ref, o_ref):
    tq = q_ref.shape[1]
    i = pl.program_id(2)
    q = q_ref[0]
    row_c = lax.broadcasted_iota(jnp.int32, (tq, tq), 0) // CHUNK
    col_c = lax.broadcasted_iota(jnp.int32, (tq, tq), 1) // CHUNK
    diag_ok = col_c <= row_c

    def step(j, carry, masked):
        m, l, acc = carry
        r0 = pl.multiple_of(j * tq, tq)
        k = k_ref[0, pl.ds(r0, tq), :]
        v = v_ref[0, pl.ds(r0, tq), :]
        s = lax.dot_general(q, k, (((1,), (1,)), ((), ())), preferred_element_type=F32)
        if masked:
            s = jnp.where(diag_ok, s, -jnp.inf)
        m_new = jnp.maximum(m, jnp.max(s, axis=-1, keepdims=True))
        alpha = jnp.exp(m - m_new)
        p = jnp.exp(s - m_new)
        l = alpha * l + jnp.sum(p, axis=-1, keepdims=True)
        acc = alpha * acc + jnp.dot(p.astype(BF16), v, preferred_element_type=F32)
        return m_new, l, acc

    init = (jnp.full((tq, 1), -jnp.inf, F32), jnp.zeros((tq, 1), F32), jnp.zeros((tq, MLA_VDIM), F32))
    carry = lax.fori_loop(0, i, lambda j, c: step(j, c, False), init)
    _, l, acc = step(i, carry, True)
    o_ref[0] = (acc / l).astype(o_ref.dtype)


def _mla_attn(q, k, v):
    B, S, _ = q.shape
    tq = TQ_MLA
    return pl.pallas_call(
        _mla_attn_kernel,
        grid=(B, MLA_HEADS, S // tq),
        in_specs=[
            pl.BlockSpec((1, tq, MLA_QK_PAD), lambda b, h, i: (b, i, h)),
            pl.BlockSpec((1, S, MLA_QK_PAD), lambda b, h, i: (b, 0, h)),
            pl.BlockSpec((1, S, MLA_VDIM), lambda b, h, i: (b, 0, h)),
        ],
        out_specs=pl.BlockSpec((1, tq, MLA_VDIM), lambda b, h, i: (b, i, h)),
        out_shape=jax.ShapeDtypeStruct((B, S, MLA_HEADS * MLA_VDIM), BF16),
        compiler_params=_cparams(("parallel", "parallel", "arbitrary")),
    )(q, k, v)


def _band_attn_kernel(q_ref, k_ref, v_ref, bias_ref, o_ref, *, scale):
    tm = q_ref.shape[1]
    hd = q_ref.shape[2] // CA_HEADS
    W = (CA_LEFT + 1) * CHUNK
    i = pl.program_id(1)
    wchunk = lax.broadcasted_iota(jnp.int32, (CHUNK, W), 1) // CHUNK
    for c in range(tm // CHUNK):
        n = i * (tm // CHUNK) + c
        r0 = pl.multiple_of(n * CHUNK, CHUNK)
        valid = wchunk >= CA_LEFT - n
        for h in range(CA_HEADS):
            q = q_ref[0, c * CHUNK:(c + 1) * CHUNK, h * hd:(h + 1) * hd]
            k = k_ref[0, pl.ds(r0, W), h * hd:(h + 1) * hd]
            v = v_ref[0, pl.ds(r0, W), h * hd:(h + 1) * hd]
            s = lax.dot_general(q, k, (((1,), (1,)), ((), ())), preferred_element_type=F32) * scale
            s = jnp.where(valid, s + bias_ref[h], -jnp.inf)
            m = jnp.max(s, axis=-1, keepdims=True)
            p = jnp.exp(s - m)
            l = jnp.sum(p, axis=-1, keepdims=True)
            o = jnp.dot(p.astype(BF16), v, preferred_element_type=F32) / l
            o_ref[0, c * CHUNK:(c + 1) * CHUNK, h * hd:(h + 1) * hd] = o.astype(o_ref.dtype)


def _band_attn(q, kp, vp, bias):
    B, S, G = q.shape
    tm = TM_BAND
    Sp = kp.shape[1]
    hd = G // CA_HEADS
    return pl.pallas_call(
        functools.partial(_band_attn_kernel, scale=hd ** -0.5),
        grid=(B, S // tm),
        in_specs=[
            pl.BlockSpec((1, tm, G), lambda b, i: (b, i, 0)),
            pl.BlockSpec((1, Sp, G), lambda b, i: (b, 0, 0)),
            pl.BlockSpec((1, Sp, G), lambda b, i: (b, 0, 0)),
            pl.BlockSpec(bias.shape, lambda b, i: (0, 0, 0)),
        ],
        out_specs=pl.BlockSpec((1, tm, G), lambda b, i: (b, i, 0)),
        out_shape=jax.ShapeDtypeStruct((B, S, G), BF16),
        compiler_params=_cparams(("parallel", "arbitrary")),
    )(q, kp, vp, bias)


def _head_sum_matrix(n, dtype=F32):
    r = lax.broadcasted_iota(jnp.int32, (n, n), 0) // RW_HEAD
    c = lax.broadcasted_iota(jnp.int32, (n, n), 1) // RW_HEAD
    return (r == c).astype(dtype)


def _rw_prep_kernel(*refs, has_vres):
    if has_vres:
        (rkv_ref, rkvh_ref, lo_ref, loh_ref, mu_ref, w0_ref, w2_ref, a0_ref, a2_ref, g2_ref, kk_ref, ka_ref,
         vf_ref, v0_ref, v2_ref, r_o, k_o, v_o, kk_o, a_o, lw_o, g_o) = refs
    else:
        (rkv_ref, rkvh_ref, lo_ref, loh_ref, mu_ref, w0_ref, w2_ref, a0_ref, a2_ref, g2_ref, kk_ref, ka_ref,
         r_o, k_o, v_o, kk_o, a_o, lw_o, g_o) = refs
    G = r_o.shape[-1]
    tm = r_o.shape[1]
    i = pl.program_id(1)
    first = lax.broadcasted_iota(jnp.int32, (tm, 1), 0) == 0

    def shifted(cur, halo):
        prev = jnp.where(i > 0, halo[7:8, :], 0.0)
        return jnp.where(first, prev, pltpu.roll(cur, 1, axis=0))

    rkv = rkv_ref[0]
    rkv = rkv + (shifted(rkv, rkvh_ref[0]) - rkv) * mu_ref[...]
    r = rkv[:, :G]
    k = rkv[:, G:2 * G]
    v = rkv[:, 2 * G:]

    lo = lo_ref[0]
    LP = sum(LORA_PAD)
    lin = lo[:, :LP] + shifted(lo[:, LP:], loh_ref[0][:, LP:])
    o1 = LORA_PAD[0]
    o2 = o1 + LORA_PAD[1]
    o3 = o2 + LORA_PAD[2]
    w_raw = w0_ref[...] + jnp.dot(jnp.tanh(lin[:, :o1]).astype(BF16), w2_ref[...], preferred_element_type=F32)
    w = -(jnp.maximum(-w_raw, 0.0) + jnp.log(1.0 + jnp.exp(-jnp.abs(w_raw)))) - 0.5
    lw_o[0] = -jnp.exp(w)
    a = _sigmoid(a0_ref[...] + jnp.dot(lin[:, o1:o2].astype(BF16), a2_ref[...], preferred_element_type=F32))
    g_o[0] = jnp.dot(_sigmoid(lin[:, o2:o3]).astype(BF16), g2_ref[...], preferred_element_type=F32)
    kk = k * kk_ref[...]
    n2 = jnp.dot(kk * kk, _head_sum_matrix(G), precision=HIGHEST, preferred_element_type=F32)
    kk_o[0] = kk / jnp.maximum(jnp.sqrt(n2), 1e-12)
    k_o[0] = k * (1.0 + (a - 1.0) * ka_ref[...])
    if has_vres:
        mix = _sigmoid(v0_ref[...] + jnp.dot(lin[:, o3:].astype(BF16), v2_ref[...], preferred_element_type=F32))
        v = v + (vf_ref[0] - v) * mix
    r_o[0] = r
    v_o[0] = v
    a_o[0] = a


def _rw_prep(rkv, lora, mu_rkv, w0, w2p, a0, a2p, g2, k_k, k_a, vres):
    B, S, G3 = rkv.shape
    G = G3 // 3
    LW = lora.shape[-1]
    tm = TM_PREP
    nh = tm // 8
    row = lambda a: a.reshape(1, -1)
    full = lambda a: pl.BlockSpec(a.shape, lambda b, i: (0,) * a.ndim)
    halo = lambda w: pl.BlockSpec((1, 8, w), lambda b, i: (b, jnp.maximum(i * nh - 1, 0), 0))
    tile = lambda w: pl.BlockSpec((1, tm, w), lambda b, i: (b, i, 0))
    args = [rkv, rkv, lora, lora, row(mu_rkv), row(w0), w2p, row(a0), a2p, g2, row(k_k), row(k_a)]
    specs = [tile(G3), halo(G3), tile(LW), halo(LW)] + [full(a) for a in args[4:]]
    if vres is not None:
        v_first, v0, v2p = vres
        extra = [v_first, row(v0), v2p]
        args += extra
        specs += [tile(G), full(extra[1]), full(extra[2])]
    return pl.pallas_call(
        functools.partial(_rw_prep_kernel, has_vres=vres is not None),
        grid=(B, S // tm),
        in_specs=specs,
        out_specs=[tile(G)] * 7,
        out_shape=[jax.ShapeDtypeStruct((B, S, G), F32)] * 7,
        compiler_params=_cparams(("parallel", "parallel")),
    )(*args)


def _rw_group(r, k, v, kk, a, lw, st0, prec):
    C, N = r.shape
    H = N // RW_HEAD
    dg = lambda x, y, dims: lax.dot_general(x.astype(BF16), y.astype(BF16), (dims, ((), ())), preferred_element_type=F32)
    mm = lambda x, y: dg(x, y, ((1,), (0,)))
    nt = lambda x, y: dg(x, y, ((1,), (1,)))
    tn = lambda x, y: dg(x.T, y, ((1,), (0,)))
    mm_state = lambda x, y: jnp.dot(x, y, precision=prec, preferred_element_type=F32)

    tr = lax.broadcasted_iota(jnp.int32, (C, C), 0)
    tc = lax.broadcasted_iota(jnp.int32, (C, C), 1)
    cum = jnp.dot((tc <= tr).astype(F32), lw, precision=HIGHEST, preferred_element_type=F32)
    tot = cum[C - 1:C, :]
    e_pos = jnp.exp(cum)
    e_neg = jnp.exp(-cum)
    e_end = jnp.exp(tot - cum)
    b = kk * a
    Aa = jnp.exp(cum - lw) * kk
    Bb = e_neg * b
    Kk = e_neg * k
    Rr = e_pos * r
    Bh = e_end * b
    Kh = e_end * k

    sr = lax.broadcasted_iota(jnp.int32, (H * C, N), 0)
    sc = lax.broadcasted_iota(jnp.int32, (H * C, N), 1)
    hmask = (sr // C) == (sc // RW_HEAD)
    tile = lambda x: jnp.concatenate([x] * H, axis=0)
    stack = lambda x: jnp.where(hmask, tile(x), 0.0)
    collapse = lambda xs: sum(xs[h * C:(h + 1) * C, :] for h in range(H))
    As, Rs, Vs = stack(Aa), stack(Rr), stack(v)
    Bt, Kt = tile(Bb), tile(Kk)

    br = lax.broadcasted_iota(jnp.int32, (H * C, H * C), 0)
    bc = lax.broadcasted_iota(jnp.int32, (H * C, H * C), 1)
    same = (br // C) == (bc // C)
    strict = same & ((br % C) > (bc % C))
    incl = same & ((br % C) >= (bc % C))
    eye = br == bc
    Lab = jnp.where(strict, nt(As, Bt), 0.0)
    Lak = jnp.where(strict, nt(As, Kt), 0.0)
    Mrb = jnp.where(incl, nt(Rs, Bt), 0.0)
    Mrk = jnp.where(incl, nt(Rs, Kt), 0.0)

    M = -Lab
    X = jnp.where(eye, 1.0, M)
    P = mm(M, M)
    steps = int(np.log2(C)) - 1
    for s in range(steps):
        X = X + mm(X, P)
        if s + 1 < steps:
            P = mm(P, P)
    W1s = mm(X, As)
    W2s = mm(X, mm(Lak, Vs))
    Q = collapse(Rs - mm(Mrb, W1s))
    Z = collapse(mm(Mrk, Vs) - mm(Mrb, W2s))
    W1 = collapse(W1s)
    W2 = collapse(W2s)
    kr = lax.broadcasted_iota(jnp.int32, (N, N), 0)
    kc = lax.broadcasted_iota(jnp.int32, (N, N), 1)
    bd = (kr // RW_HEAD) == (kc // RW_HEAD)
    Gm = jnp.where(kr == kc, jnp.exp(tot), 0.0) - jnp.where(bd, tn(Bh, W1), 0.0)
    Hm = jnp.where(bd, tn(Kh, v) - tn(Bh, W2), 0.0)
    y = mm_state(Q, st0) + Z
    st1 = mm_state(Gm, st0) + Hm
    return y, st1


def _rw_scan_kernel(r_ref, k_ref, v_ref, kk_ref, a_ref, lw_ref, g_ref, rk_ref, gg_ref, gb_ref, o_ref, st_ref, *, prec):
    n = pl.program_id(1)
    N = RW_GROUP_HEADS * RW_HEAD

    @pl.when(n == 0)
    def _():
        st_ref[...] = jnp.zeros_like(st_ref)

    hs = _head_sum_matrix(N)
    for gi in range(r_ref.shape[-1] // N):
        sl = slice(gi * N, (gi + 1) * N)
        r, k, v = r_ref[0, :, sl], k_ref[0, :, sl], v_ref[0, :, sl]
        y, st1 = _rw_group(r, k, v, kk_ref[0, :, sl], a_ref[0, :, sl], lw_ref[0, :, sl], st_ref[gi], prec)
        st_ref[gi] = st1
        hmean = lambda t: jnp.dot(t, hs, precision=HIGHEST, preferred_element_type=F32) * (1.0 / RW_HEAD)
        yc = y - hmean(y)
        yn = yc * lax.rsqrt(hmean(yc * yc) + RW_GN_EPS) * gg_ref[:, sl] + gb_ref[:, sl]
        bonus = jnp.dot(r * k * rk_ref[:, sl], hs, precision=HIGHEST, preferred_element_type=F32) * v
        o_ref[0, :, sl] = ((yn + bonus) * g_ref[0, :, sl]).astype(o_ref.dtype)


def _rw_scan(r, k, v, kk, a, lw, g, r_k, gn_g, gn_b, prec):
    B, S, G = r.shape
    N = RW_GROUP_HEADS * RW_HEAD
    tile = pl.BlockSpec((1, CHUNK, G), lambda b, n: (b, n, 0))
    rowspec = pl.BlockSpec((1, G), lambda b, n: (0, 0))
    return pl.pallas_call(
        functools.partial(_rw_scan_kernel, prec=prec),
        grid=(B, S // CHUNK),
        in_specs=[tile] * 7 + [rowspec] * 3,
        out_specs=tile,
        out_shape=jax.ShapeDtypeStruct((B, S, G), BF16),
        scratch_shapes=[pltpu.VMEM((G // N, N, N), F32)],
        compiler_params=_cparams(("parallel", "arbitrary")),
    )(r, k, v, kk, a, lw, g, r_k.reshape(1, G), gn_g.reshape(1, G), gn_b.reshape(1, G))


def _out_proj_kernel(yc_ref, ym_ref, yr_ref, ya_ref, w_ref, x_ref, g1_ref, lg_ref, lb_ref, sc_ref, sh_ref,
                     rw_ref, rb_ref, x1_ref, h2_ref, lo_ref):
    G = yc_ref.shape[-1]
    y = jnp.zeros(x_ref.shape[1:], F32)
    for j, y_ref in enumerate((yc_ref, ym_ref, yr_ref, ya_ref)):
        y = y + jnp.dot(y_ref[0], w_ref[j * G:(j + 1) * G, :], preferred_element_type=F32)
    x1 = _ln_rows(DN_ALPHA * x_ref[0] + (1.0 + g1_ref[0]) * y) * lg_ref[...] + lb_ref[...]
    x1_ref[0] = x1
    h2 = _ln_rows(x1) * (1.0 + sc_ref[0]) + sh_ref[0]
    h2_ref[0] = h2.astype(h2_ref.dtype)
    lo_ref[0] = jnp.dot(h2, rw_ref[...], precision=HIGHEST, preferred_element_type=F32) + rb_ref[...]


def _out_proj(ys, w_out, x, g1, ln_g, ln_b, sc2, sh2, router_w, router_b):
    B, S, D = x.shape
    G = ys[0].shape[-1]
    E = router_w.shape[-1]
    tm = TM_OUT
    tile = lambda w: pl.BlockSpec((1, tm, w), lambda b, i: (b, i, 0))
    mod = pl.BlockSpec((1, 1, D), lambda b, i: (b, 0, 0))
    full = lambda a: pl.BlockSpec(a.shape, lambda b, i: (0,) * a.ndim)
    lg, lb, rb = ln_g.reshape(1, D), ln_b.reshape(1, D), router_b.reshape(1, E)
    return pl.pallas_call(
        _out_proj_kernel,
        grid=(B, S // tm),
        in_specs=[tile(G)] * 4 + [pl.BlockSpec(memory_space=pltpu.VMEM), tile(D), mod, full(lg), full(lb), mod, mod,
                                  full(router_w), full(rb)],
        out_specs=[tile(D), tile(D), tile(E)],
        out_shape=[jax.ShapeDtypeStruct((B, S, D), F32), jax.ShapeDtypeStruct((B, S, D), BF16),
                   jax.ShapeDtypeStruct((B, S, E), F32)],
        compiler_params=_cparams(("parallel", "parallel")),
    )(*ys, w_out, x, g1, lg, lb, sc2, sh2, router_w, rb)


def _moe_kernel(te_ref, nt_ref, x_ref, wg_ref, wl_ref, bg_ref, bl_ref, w2_ref, b2_ref, sw_ref, o_ref, acc_ref):
    i = pl.program_id(0)
    f = pl.program_id(1)
    nf = pl.num_programs(1)
    live = i < nt_ref[0]

    @pl.when(f == 0)
    def _():
        acc_ref[...] = jnp.zeros_like(acc_ref)

    @pl.when(live)
    def _():
        x = x_ref[...]
        glu = jnp.dot(x, wg_ref[0].astype(BF16), preferred_element_type=F32) + bg_ref[0]
        lin = jnp.dot(x, wl_ref[0].astype(BF16), preferred_element_type=F32) + bl_ref[0]
        glu = jnp.minimum(glu, SWIGLU_LIMIT)
        lin = jnp.clip(lin, -SWIGLU_LIMIT, SWIGLU_LIMIT)
        act = glu * _sigmoid(SWIGLU_ALPHA * glu) * (lin + 1.0)
        acc_ref[...] += jnp.dot(act.astype(BF16), w2_ref[0].astype(BF16), preferred_element_type=F32)

    @pl.when(f == nf - 1)
    def _():
        o_ref[...] = ((acc_ref[...] + b2_ref[0]) * sw_ref[...]).astype(o_ref.dtype)


def _moe_experts(tile_e, n_live, xs, w1, b1, w2, b2, slot_w, layer):
    n_slots, D = xs.shape
    L, E, _, F2 = w1.shape
    F = F2 // 2
    tm, tf = TM_MOE, TF_MOE
    nf = F // tf
    w1 = w1.reshape(L * E, D, F2)
    w2 = w2.reshape(L * E, F, D)
    b1r = b1.reshape(L * E, 1, F2)
    b2r = b2.reshape(L * E, 1, D)
    tile_e = tile_e + layer * E
    fidx = lambda i, f, nt: jnp.where(i < nt[0], f, nf - 1)
    grid_spec = pltpu.PrefetchScalarGridSpec(
        num_scalar_prefetch=2,
        grid=(n_slots // tm, nf),
        in_specs=[
            pl.BlockSpec((tm, D), lambda i, f, te, nt: (i, 0)),
            pl.BlockSpec((1, D, tf), lambda i, f, te, nt: (te[i], 0, fidx(i, f, nt))),
            pl.BlockSpec((1, D, tf), lambda i, f, te, nt: (te[i], 0, nf + fidx(i, f, nt))),
            pl.BlockSpec((1, 1, tf), lambda i, f, te, nt: (te[i], 0, fidx(i, f, nt))),
            pl.BlockSpec((1, 1, tf), lambda i, f, te, nt: (te[i], 0, nf + fidx(i, f, nt))),
            pl.BlockSpec((1, tf, D), lambda i, f, te, nt: (te[i], fidx(i, f, nt), 0)),
            pl.BlockSpec((1, 1, D), lambda i, f, te, nt: (te[i], 0, 0)),
            pl.BlockSpec((tm, 1), lambda i, f, te, nt: (i, 0)),
        ],
        out_specs=pl.BlockSpec((tm, D), lambda i, f, te, nt: (i, 0)),
        scratch_shapes=[pltpu.VMEM((tm, D), F32)],
    )
    return pl.pallas_call(
        _moe_kernel,
        grid_spec=grid_spec,
        out_shape=jax.ShapeDtypeStruct((n_slots, D), F32),
        compiler_params=_cparams(("arbitrary", "arbitrary")),
    )(tile_e, n_live, xs, w1, w1, b1r, b1r, w2, b2r, slot_w.reshape(n_slots, 1))


def _final_kernel(x_ref, y0_ref, y1_ref, y2_ref, y3_ref, g2_ref, lg_ref, lb_ref, o_ref):
    y = (y0_ref[0] + y1_ref[0]) + (y2_ref[0] + y3_ref[0])
    o_ref[0] = _ln_rows(DN_ALPHA * x_ref[0] + (1.0 + g2_ref[0]) * y) * lg_ref[...] + lb_ref[...]


def _final(x1, ys, g2, ln_g, ln_b):
    B, S, D = x1.shape
    tm = TM_FIN
    tile = pl.BlockSpec((1, tm, D), lambda b, i: (b, i, 0))
    return pl.pallas_call(
        _final_kernel,
        grid=(B, S // tm),
        in_specs=[tile] * (1 + TOP_K) + [
            pl.BlockSpec((1, 1, D), lambda b, i: (b, 0, 0)),
            pl.BlockSpec((1, D), lambda b, i: (0, 0)),
            pl.BlockSpec((1, D), lambda b, i: (0, 0)),
        ],
        out_specs=tile,
        out_shape=jax.ShapeDtypeStruct((B, S, D), F32),
        compiler_params=_cparams(("parallel", "parallel")),
    )(x1, *ys, g2, ln_g.reshape(1, D), ln_b.reshape(1, D))


def _pad_cols(w, n):
    return jnp.pad(w, ((0, 0), (0, n - w.shape[1])))


def _pad_rows(w, n):
    return jnp.pad(w, ((0, n - w.shape[0]), (0, 0)))


def _lora_fold(mu, w1, n):
    return _pad_cols((1.0 - mu)[:, None] * w1, n), _pad_cols(mu[:, None] * w1, n)


def _mla_weights(w_uq, w_ukv):
    Rq = w_uq.shape[0]
    W = MLA_QK_PAD
    half = MLA_ROPE // 2
    uq = w_uq.reshape(Rq, MLA_HEADS, MLA_NOPE + MLA_ROPE)
    pe = uq[:, :, MLA_NOPE:]
    pe_rot = jnp.concatenate([-pe[:, :, half:], pe[:, :, :half]], axis=-1)
    zpad = jnp.zeros((Rq, MLA_HEADS, W - MLA_NOPE - MLA_ROPE), F32)
    wqa = jnp.concatenate([uq, zpad], axis=-1).reshape(Rq, MLA_HEADS * W)
    wqr = jnp.concatenate([jnp.zeros_like(uq[:, :, :MLA_NOPE]), pe_rot, zpad], axis=-1).reshape(Rq, MLA_HEADS * W)
    Rk = w_ukv.shape[0]
    ukv = w_ukv.reshape(Rk, MLA_HEADS, MLA_NOPE + MLA_VDIM)
    wkn = jnp.concatenate([ukv[:, :, :MLA_NOPE], jnp.zeros((Rk, MLA_HEADS, W - MLA_NOPE), F32)], axis=-1)
    wkn = wkn.reshape(Rk, MLA_HEADS * W)
    wv = ukv[:, :, MLA_NOPE:].reshape(Rk, MLA_HEADS * MLA_VDIM)
    pa = np.zeros((128, W), np.float32)
    pr = np.zeros((128, W), np.float32)
    for d in range(MLA_ROPE):
        pa[d, MLA_NOPE + d] = 1.0
    for d in range(half):
        pr[half + d, MLA_NOPE + d] = -1.0
        pr[d, MLA_NOPE + half + d] = 1.0
    return wqa.astype(BF16), wqr.astype(BF16), wkn.astype(BF16), wv.astype(BF16), jnp.asarray(pa), jnp.asarray(pr)


def _rope_tables(positions):
    B, S = positions.shape
    inv = ROPE_THETA ** (-jnp.arange(0, MLA_ROPE, 2, dtype=F32) / MLA_ROPE)
    ang = positions.astype(F32)[..., None] * inv
    cos, sin = jnp.cos(ang), jnp.sin(ang)
    one = jnp.ones((B, S, MLA_NOPE), F32)
    zero = jnp.zeros((B, S, MLA_NOPE), F32)
    zpad = jnp.zeros((B, S, MLA_QK_PAD - MLA_NOPE - MLA_ROPE), F32)
    return (jnp.concatenate([one, cos, cos, zpad], -1), jnp.concatenate([zero, sin, sin, zpad], -1))


def _band_bias(rel_table):
    W = (CA_LEFT + 1) * CHUNK
    q_pos = CA_LEFT * CHUNK + jnp.arange(CHUNK)[:, None]
    k_pos = jnp.arange(W)[None, :]
    rel = jnp.clip(q_pos - k_pos, -REL_CLIP, REL_CLIP) + REL_CLIP
    return rel_table[:, rel].astype(F32)


def _route(logits, tm):
    T = logits.shape[0]
    top_val, top_idx = lax.top_k(logits, TOP_K)
    gate = jax.nn.softmax(top_val, axis=-1)
    flat_e = top_idx.reshape(-1)
    flat_tok = jnp.repeat(jnp.arange(T, dtype=jnp.int32), TOP_K)
    order = jnp.argsort(flat_e)
    e_sorted = flat_e[order]
    counts = jnp.bincount(flat_e, length=N_EXPERTS)
    padded = (counts + tm - 1) // tm * tm
    start = jnp.cumsum(counts) - counts
    pad_end = jnp.cumsum(padded)
    pad_start = pad_end - padded
    dest = (pad_start[e_sorted] + jnp.arange(T * TOP_K) - start[e_sorted]).astype(jnp.int32)
    n_tiles = -(-(T * TOP_K + N_EXPERTS * (tm - 1)) // tm)
    n_slots = n_tiles * tm
    slot_tok = jnp.zeros((n_slots,), jnp.int32).at[dest].set(flat_tok[order])
    slot_w = jnp.zeros((n_slots,), F32).at[dest].set(gate.reshape(-1)[order])
    pos = jnp.zeros((T * TOP_K,), jnp.int32).at[order].set(dest)
    n_live = (pad_end[-1] // tm).astype(jnp.int32)
    tile_e = jnp.minimum(jnp.searchsorted(pad_end, jnp.arange(n_tiles) * tm, side='right'), N_EXPERTS - 1)
    last_e = tile_e[jnp.maximum(n_live - 1, 0)]
    tile_e = jnp.where(jnp.arange(n_tiles) < n_live, tile_e, last_e).astype(jnp.int32)
    return slot_tok, slot_w, pos, tile_e, n_live.reshape(1)


def kernel(x, c, positions, mod_w, mod_b, mod_offset, ln_post_g, ln_post_b, w_in, w_out, conv_dw, conv_b, conv_ln_g, conv_ln_b, mla_q_norm, mla_kv_norm, mla_w_uq, mla_w_ukv, rw_mu_rkv, rw_mu_x, rw_w0, rw_w1, rw_w2, rw_a0, rw_a1, rw_a2, rw_g1, rw_g2, rw_k_k, rw_k_a, rw_r_k, rw_gn_g, rw_gn_b, rw_vres_mu, rw_v0, rw_v1, rw_v2, ca_rel_bias, moe_router_w, moe_router_b, moe_w1, moe_b1, moe_w2, moe_b2):
    B, S, D = x.shape
    G = D // 4
    L = w_in.shape[0]
    T = B * S
    ctab, stab = _rope_tables(positions)
    mod_shared = jax.nn.silu(c) @ mod_w + mod_b
    v_first = None
    for l in range(L):
        mod = mod_shared + mod_offset[l]
        sh1, sc1, g1, sh2, sc2, g2 = [m[:, None, :] for m in jnp.split(mod, N_MOD, axis=-1)]

        wi = w_in[l]
        o = np.cumsum((0, 2 * G, MLA_Q_RANK, MLA_KV_RANK, MLA_ROPE, 3 * G, 3 * G))
        a_parts, b_parts = [], []
        loras = [(rw_mu_x[l, 0], rw_w1[l]), (rw_mu_x[l, 1], rw_a1[l]), (rw_mu_x[l, 2], rw_g1[l])]
        if l > 0:
            loras.append((rw_vres_mu[l - 1], rw_v1[l - 1]))
        for (mu, w1), n in zip(loras, LORA_PAD):
            fa, fb = _lora_fold(mu, w1, n)
            a_parts.append(fa)
            b_parts.append(fb)
        if l == 0:
            a_parts.append(jnp.zeros((D, LORA_PAD[3]), F32))
            b_parts.append(jnp.zeros((D, LORA_PAD[3]), F32))
        w_mla = _pad_cols(wi[:, o[1]:o[4]], 768)
        w_big = jnp.concatenate([wi[:, o[0]:o[1]], w_mla, wi[:, o[4]:o[5]], wi[:, o[5]:o[6]]] + a_parts + b_parts,
                                axis=1).astype(BF16)
        widths = (2 * G, 768, 3 * G, 3 * G, 2 * sum(LORA_PAD))
        u_conv, mla_in, rkv, qkv, lora = _in_proj(x, sc1, sh1, w_big, widths)

        y_conv = _conv_module(u_conv, conv_dw[l], conv_b[l], conv_ln_g[l], conv_ln_b[l])

        wqa, wqr, wkn, wv, pa, pr = _mla_weights(mla_w_uq[l], mla_w_ukv[l])
        q_m, k_m, v_m = _mla_prep(mla_in, ctab, stab, mla_q_norm[l], mla_kv_norm[l], wqa, wqr, wkn, wv, pa, pr)
        y_mla = _mla_attn(q_m, k_m, v_m)

        w2p = _pad_rows(rw_w2[l], LORA_PAD[0]).astype(BF16)
        a2p = _pad_rows(rw_a2[l], LORA_PAD[1]).astype(BF16)
        vres = None
        if l > 0:
            vres = (v_first, rw_v0[l - 1], _pad_rows(rw_v2[l - 1], LORA_PAD[3]).astype(BF16))
        r_, k_, v_, kk_, a_, lw_, g_ = _rw_prep(rkv, lora, rw_mu_rkv[l].reshape(-1), rw_w0[l], w2p, rw_a0[l], a2p,
                                                rw_g2[l].astype(BF16), rw_k_k[l], rw_k_a[l], vres)
        if l == 0:
            v_first = v_
        y_rw = _rw_scan(r_, k_, v_, kk_, a_, lw_, g_, rw_r_k[l].reshape(-1), rw_gn_g[l], rw_gn_b[l], RW_PREC)

        pad = ((0, 0), (CA_LEFT * CHUNK, 0), (0, 0))
        qa = qkv[:, :, :G].astype(BF16)
        kp = jnp.pad(qkv[:, :, G:2 * G].astype(BF16), pad)
        vp = jnp.pad(qkv[:, :, 2 * G:].astype(BF16), pad)
        y_ca = _band_attn(qa, kp, vp, _band_bias(ca_rel_bias[l]))

        x1, h2, logits = _out_proj((y_conv, y_mla, y_rw, y_ca), w_out[l].astype(BF16), x, g1,
                                   ln_post_g[l, 0], ln_post_b[l, 0], sc2, sh2, moe_router_w[l], moe_router_b[l])

        slot_tok, slot_w, pos, tile_e, n_live = _route(logits.reshape(T, N_EXPERTS), TM_MOE)
        xs = jnp.take(h2.reshape(T, D), slot_tok, axis=0)
        ys = _moe_experts(tile_e, n_live, xs, moe_w1, moe_b1, moe_w2, moe_b2, slot_w, l)
        pos_k = pos.reshape(T, TOP_K)
        yk = [jnp.take(ys, pos_k[:, j], axis=0).reshape(B, S, D) for j in range(TOP_K)]
        x = _final(x1, yk, g2, ln_post_g[l, 1], ln_post_b[l, 1])
    return x
```
